```python
import jax, jax.numpy as jnp
from jax import lax
import numpy as np

D_MODEL = 2048
BATCH = 2
SEQ = 4096
DEPTH = 1

D_MIX = D_MODEL
D_FOURIER = D_MIX // 2
N_FOURIER_GROUPS = 4
FOURIER_GROUP = D_FOURIER // N_FOURIER_GROUPS
D_MLSTM = D_MIX - D_FOURIER
N_MLSTM_HEADS = 4
HEAD_V = D_MLSTM // N_MLSTM_HEADS
HEAD_QK = HEAD_V // 2
D_QK = N_MLSTM_HEADS * HEAD_QK
N_GATES = 4 * N_MLSTM_HEADS
D_IN = D_FOURIER + 2 * D_QK + 2 * D_MLSTM + N_GATES
CHUNK = 64
D_FF = 5632
CONV_W = 3
N_MOD = 6
EPS = 1e-6
M_INIT = -1e30

kernel_name = "hybrid_fnet_mlstm_convffn_adaln_block"


def rms_norm(x, w):
    xf = x.astype(jnp.float32)
    y = xf * lax.rsqrt(jnp.mean(xf * xf, axis=-1, keepdims=True) + EPS)
    return (y * w.astype(jnp.float32)).astype(x.dtype)


def modulate(h, shift, scale):
    return h * (1 + scale[:, None, :]) + shift[:, None, :]


def fourier_mix(u):
    B, S, _ = u.shape
    ug = u.astype(jnp.float32).reshape(B, S, N_FOURIER_GROUPS, FOURIER_GROUP)
    y = jnp.fft.fft2(ug, axes=(1, 3), norm="ortho").real
    return y.reshape(B, S, D_FOURIER).astype(u.dtype)


def mlstm_chunkwise(q, k, v, ig, fg):
    B, H, S, Dk = q.shape
    Dv = v.shape[-1]
    NC = S // CHUNK
    lf = jax.nn.log_sigmoid(fg)
    q = q.reshape(B, H, NC, CHUNK, Dk) * (Dk ** -0.5)
    k = k.reshape(B, H, NC, CHUNK, Dk)
    v = v.reshape(B, H, NC, CHUNK, Dv)
    ig = ig.reshape(B, H, NC, CHUNK)
    lf = lf.reshape(B, H, NC, CHUNK)
    b = jnp.cumsum(lf, axis=-1)
    g = b[..., -1]

    a = g[..., None] - b + ig
    m_loc = jnp.max(a, axis=-1)
    w_loc = jnp.exp(a - m_loc[..., None])
    C_loc = jnp.einsum('bhclv,bhclk->bhcvk', v * w_loc[..., None], k)
    n_loc = jnp.einsum('bhcl,bhclk->bhck', w_loc, k)

    def step(carry, inp):
        C, n, m = carry
        g_c, m_c, C_c, n_c = inp
        m_new = jnp.maximum(g_c + m, m_c)
        s_old = jnp.exp(g_c + m - m_new)
        s_loc = jnp.exp(m_c - m_new)
        C_new = s_old[..., None, None] * C + s_loc[..., None, None] * C_c
        n_new = s_old[..., None] * n + s_loc[..., None] * n_c
        return (C_new, n_new, m_new), (C, n, m)

    init = (jnp.zeros((B, H, Dv, Dk), jnp.float32),
            jnp.zeros((B, H, Dk), jnp.float32),
            jnp.full((B, H), M_INIT, jnp.float32))
    xs = (jnp.moveaxis(g, 2, 0), jnp.moveaxis(m_loc, 2, 0),
          jnp.moveaxis(C_loc, 2, 0), jnp.moveaxis(n_loc, 2, 0))
    _, (C_st, n_st, m_st) = lax.scan(step, init, xs)
    C_st = jnp.moveaxis(C_st, 0, 2)
    n_st = jnp.moveaxis(n_st, 0, 2)
    m_st = jnp.moveaxis(m_st, 0, 2)

    tri = jnp.tril(jnp.ones((CHUNK, CHUNK), dtype=bool))
    Dlog = jnp.where(tri, b[..., :, None] - b[..., None, :] + ig[..., None, :], -jnp.inf)
    inter = b + m_st[..., None]
    m_row = jnp.maximum(jnp.max(Dlog, axis=-1), inter)
    P = jnp.exp(Dlog - m_row[..., None]) * jnp.einsum('bhcjd,bhctd->bhcjt', q, k)
    s_inter = jnp.exp(inter - m_row)
    num = (jnp.einsum('bhcjt,bhctv->bhcjv', P, v)
           + s_inter[..., None] * jnp.einsum('bhcjk,bhcvk->bhcjv', q, C_st))
    den = jnp.sum(P, axis=-1) + s_inter * jnp.einsum('bhcjk,bhck->bhcj', q, n_st)
    h = num / jnp.maximum(jnp.abs(den), jnp.exp(-m_row))[..., None]
    return h.reshape(B, H, S, Dv)


def mlstm_bidirectional(q_in, k_in, v_in, gates, mlstm_norm_w):
    B, S, _ = q_in.shape
    H = N_MLSTM_HEADS
    q = q_in.astype(jnp.float32).reshape(B, S, H, HEAD_QK).transpose(0, 2, 1, 3)
    k = k_in.astype(jnp.float32).reshape(B, S, H, HEAD_QK).transpose(0, 2, 1, 3)
    v = v_in.astype(jnp.float32).reshape(B, S, H, HEAD_V).transpose(0, 2, 1, 3)
    gt = gates.astype(jnp.float32).reshape(B, S, 4, H).transpose(2, 0, 3, 1)
    i_f, f_f, i_b, f_b = gt[0], gt[1], gt[2], gt[3]
    h_fwd = mlstm_chunkwise(q, k, v, i_f, f_f)
    h_bwd = jnp.flip(mlstm_chunkwise(jnp.flip(q, 2), jnp.flip(k, 2), jnp.flip(v, 2),
                                     jnp.flip(i_b, 2), jnp.flip(f_b, 2)), 2)
    h = (h_fwd + h_bwd).transpose(0, 2, 1, 3)
    h = h * lax.rsqrt(jnp.mean(h * h, axis=-1, keepdims=True) + EPS)
    h = h.reshape(B, S, D_MLSTM) * mlstm_norm_w.astype(jnp.float32)
    return h.astype(q_in.dtype)


def depthwise_conv_centred(u, w_conv, b_conv):
    C = u.shape[-1]
    y = lax.conv_general_dilated(u, w_conv[:, None, :].astype(u.dtype), window_strides=(1,),
                                 padding='SAME', dimension_numbers=('NWC', 'WIO', 'NWC'),
                                 feature_group_count=C)
    return y + b_conv


def setup_inputs(seed: int = 0) -> dict:
    key = jax.random.key(seed)
    ks = jax.random.split(key, 20)
    f32 = jnp.float32
    H = N_MLSTM_HEADS
    x = jax.random.normal(ks[0], (BATCH, SEQ, D_MODEL), f32)
    c = jax.random.normal(ks[1], (BATCH, D_MODEL), f32)
    w_ada = jax.random.normal(ks[2], (D_MODEL, N_MOD * D_MODEL), f32) * (0.5 * D_MODEL ** -0.5)
    b_ada = 0.02 * jax.random.normal(ks[3], (N_MOD * D_MODEL,), f32)
    norm1_w = 1.0 + 0.02 * jax.random.normal(ks[4], (D_MODEL,), f32)
    w_in = jax.random.normal(ks[5], (D_MODEL, D_IN), f32) * (D_MODEL ** -0.5)
    bi_f = -1.0 + 0.1 * jax.random.normal(ks[6], (H,), f32)
    bf_f = jnp.linspace(3.0, 6.0, H, dtype=f32) + 0.1 * jax.random.normal(ks[7], (H,), f32)
    bi_b = -1.0 + 0.1 * jax.random.normal(ks[8], (H,), f32)
    bf_b = jnp.linspace(3.0, 6.0, H, dtype=f32) + 0.1 * jax.random.normal(ks[9], (H,), f32)
    b_gates = jnp.concatenate([bi_f, bf_f, bi_b, bf_b])
    mlstm_norm_w = 1.0 + 0.02 * jax.random.normal(ks[10], (D_MLSTM,), f32)
    w_out = jax.random.normal(ks[11], (D_MIX, D_MODEL), f32) * (D_MIX ** -0.5)
    norm2_w = 1.0 + 0.02 * jax.random.normal(ks[12], (D_MODEL,), f32)
    w_up = jax.random.normal(ks[13], (D_MODEL, 2 * D_FF), f32) * (D_MODEL ** -0.5)
    w_conv = jax.random.normal(ks[14], (CONV_W, 2 * D_FF), f32) * (CONV_W ** -0.5)
    b_conv = 0.02 * jax.random.normal(ks[15], (2 * D_FF,), f32)
    w_down = jax.random.normal(ks[16], (D_FF, D_MODEL), f32) * (D_FF ** -0.5)
    norm_f_w = 1.0 + 0.02 * jax.random.normal(ks[17], (D_MODEL,), f32)
    return {"x": x, "c": c, "w_ada": w_ada, "b_ada": b_ada, "norm1_w": norm1_w,
            "w_in": w_in, "b_gates": b_gates, "mlstm_norm_w": mlstm_norm_w, "w_out": w_out,
            "norm2_w": norm2_w, "w_up": w_up, "w_conv": w_conv, "b_conv": b_conv,
            "w_down": w_down, "norm_f_w": norm_f_w}


def reference(x, c, w_ada, b_ada, norm1_w, w_in, b_gates, mlstm_norm_w, w_out,
              norm2_w, w_up, w_conv, b_conv, w_down, norm_f_w):
    mod = jax.nn.silu(c) @ w_ada + b_ada
    shift1, scale1, gate1, shift2, scale2, gate2 = jnp.split(mod, N_MOD, axis=-1)
    o1 = D_FOURIER
    o2 = o1 + D_QK
    o3 = o2 + D_QK
    o4 = o3 + D_MLSTM
    o5 = o4 + D_MLSTM
    for _ in range(DEPTH):
        h = modulate(rms_norm(x, norm1_w), shift1, scale1)
        p = h @ w_in
        u_f = p[..., :o1]
        q_in, k_in, v_in = p[..., o1:o2], p[..., o2:o3], p[..., o3:o4]
        o_gate = jax.nn.sigmoid(p[..., o4:o5])
        gates = p[..., o5:] + b_gates
        y_f = fourier_mix(u_f)
        y_m = mlstm_bidirectional(q_in, k_in, v_in, gates, mlstm_norm_w) * o_gate
        mix = jnp.concatenate([y_f, y_m], axis=-1)
        x = x + gate1[:, None, :] * (mix @ w_out)
        h2 = modulate(rms_norm(x, norm2_w), shift2, scale2)
        u = depthwise_conv_centred(h2 @ w_up, w_conv, b_conv)
        a, v = jnp.split(u, 2, axis=-1)
        x = x + gate2[:, None, :] * ((jax.nn.silu(a) * v) @ w_down)
    return rms_norm(x, norm_f_w)
```

```python
import functools

import numpy as np
import jax
import jax.numpy as jnp
from jax import lax
from jax.experimental import pallas as pl
from jax.experimental.pallas import tpu as pltpu

F32 = jnp.float32
BF16 = jnp.bfloat16

N_MOD = 6
N_FOURIER_GROUPS = 4
N_HEADS = 4
RADIX = 4
EPS = 1e-6
M_INIT = -1e30
MLSTM_CHUNK = 128
N_GATE_LANES = 128
HALO = 16
V7X_VMEM_LIMIT = 56 * 1024 * 1024


def _params(sem, vmem=V7X_VMEM_LIMIT):
    return pltpu.CompilerParams(dimension_semantics=sem, vmem_limit_bytes=vmem)


def _resident(block_shape, index_map):
    return pl.BlockSpec(block_shape, index_map, pipeline_mode=pl.Buffered(1))


def _adaln_kernel(c_ref, w_ref, b_ref, o_ref):
    c = c_ref[...]
    s = c * jax.nn.sigmoid(c)
    s_hi = s.astype(BF16).astype(F32)
    row = lax.broadcasted_iota(jnp.int32, s.shape, 0)
    lhs = jnp.where(row < 8, s_hi, s - s_hi).astype(BF16)
    w = w_ref[...]
    w_hi = w.astype(BF16)
    w_lo = (w - w_hi.astype(F32)).astype(BF16)
    r = jnp.dot(lhs, w_hi, preferred_element_type=F32)
    r = r + jnp.dot(lhs, w_lo, preferred_element_type=F32)
    o_ref[...] = r[0:8] + r[8:16] + b_ref[...]


def _adaln(c, w_ada, b_ada):
    B, D = c.shape
    N = w_ada.shape[1]
    tn = min(1024, D)
    cp =jnp.zeros((16, D), F32).at[0:B].set(c).at[8:8 + B].set(c)
    out = pl.pallas_call(
        _adaln_kernel,
        grid=(N // tn,),
        in_specs=[pl.BlockSpec((16, D), lambda j: (0, 0)),
                  pl.BlockSpec((D, tn), lambda j: (0, j)),
                  pl.BlockSpec((1, tn), lambda j: (0, j))],
        out_specs=pl.BlockSpec((8, tn), lambda j: (0, j)),
        out_shape=jax.ShapeDtypeStruct((8, N), F32),
        compiler_params=_params(("arbitrary",)),
        name="adaln",
    )(cp, w_ada, b_ada.reshape(1, N))
    return out[:B].reshape(B, N_MOD, D)


def _rms_mod(x, nw, shift, scale):
    ms = jnp.mean(x * x, axis=-1, keepdims=True)
    return x * lax.rsqrt(ms + EPS) * (nw * (1.0 + scale)) + shift


def _inproj_kernel(x_ref, mod_ref, nw_ref, w_ref, wg_ref, bg_ref,
                   uf_ref, q_ref, k_ref, v_ref, og_ref, g_ref, *, dims, qscale):
    d_f, d_qk, d_m = dims
    h = _rms_mod(x_ref[...], nw_ref[...], mod_ref[0:1, :], mod_ref[1:2, :]).astype(BF16)

    def proj(c0, width):
        return jnp.dot(h, w_ref[:, c0:c0 + width], preferred_element_type=F32)

    cw = min(512, d_qk)
    for c0 in range(0, d_f, cw):
        uf_ref[:, c0:c0 + cw] = proj(c0, cw).astype(BF16)
    o = d_f
    for c0 in range(0, d_qk, cw):
        q_ref[:, c0:c0 + cw] = (proj(o + c0, cw) * qscale).astype(BF16)
    o += d_qk
    for c0 in range(0, d_qk, cw):
        k_ref[:, c0:c0 + cw] = proj(o + c0, cw).astype(BF16)
    o += d_qk
    for c0 in range(0, d_m, cw):
        v_ref[:, c0:c0 + cw] = proj(o + c0, cw).astype(BF16)
    o += d_m
    for c0 in range(0, d_m, cw):
        og_ref[:, c0:c0 + cw] = jax.nn.sigmoid(proj(o + c0, cw)).astype(BF16)
    g_ref[...] = jnp.dot(h, wg_ref[...], preferred_element_type=F32) + bg_ref[...]


def _inproj(x2, mod, norm1_w, w_main, w_gate, b_gate, S, dims):
    T, D = x2.shape
    d_f, d_qk, d_m = dims
    tm = min(512, S)
    tiles_per_seq = S // tm
    n_main = w_main.shape[1]
    tok = lambda width: pl.BlockSpec((tm, width), lambda i: (i, 0))
    kern = functools.partial(_inproj_kernel, dims=dims,
                             qscale=float((d_qk // N_HEADS) ** -0.5))
    return pl.pallas_call(
        kern,
        grid=(T // tm,),
        in_specs=[tok(D),
                  pl.BlockSpec((None, N_MOD, D), lambda i: (i // tiles_per_seq, 0, 0)),
                  pl.BlockSpec((1, D), lambda i: (0, 0)),
                  _resident((D, n_main), lambda i: (0, 0)),
                  _resident((D, N_GATE_LANES), lambda i: (0, 0)),
                  pl.BlockSpec((1, N_GATE_LANES), lambda i: (0, 0))],
        out_specs=[tok(d_f), tok(d_qk), tok(d_qk), tok(d_m), tok(d_m), tok(N_GATE_LANES)],
        out_shape=[jax.ShapeDtypeStruct((T, d_f), BF16),
                   jax.ShapeDtypeStruct((T, d_qk), BF16),
                   jax.ShapeDtypeStruct((T, d_qk), BF16),
                   jax.ShapeDtypeStruct((T, d_m), BF16),
                   jax.ShapeDtypeStruct((T, d_m), BF16),
                   jax.ShapeDtypeStruct((T, N_GATE_LANES), F32)],
        compiler_params=_params(("arbitrary",)),
        name="inproj",
    )(x2, mod, norm1_w.reshape(1, D), w_main, w_gate, b_gate)


def _dft_constants(S, gc, tk):
    n1 = S // RADIX
    idx = np.arange(n1, dtype=np.int64)
    ang = 2.0 * np.pi * ((idx[:, None] * idx[None, :]) % n1).astype(np.float64) / n1
    c, s = np.cos(ang), np.sin(ang)
    nt = n1 // tk
    cs = np.concatenate([c.reshape(nt, tk, n1), s.reshape(nt, tk, n1)], axis=1)
    tw = []
    for r in range(1, RADIX):
        a = 2.0 * np.pi * ((r * idx) % S).astype(np.float64) / S
        tw += [np.cos(a), np.sin(a)]
    tw = np.broadcast_to(np.stack(tw)[:, :, None], (2 * (RADIX - 1), n1, 128))
    ch = np.arange(gc, dtype=np.int64)
    angc = 2.0 * np.pi * ((ch[:, None] * ch[None, :]) % gc).astype(np.float64) / gc
    wch = np.concatenate([np.cos(angc), np.sin(angc)], axis=0) / np.sqrt(float(S) * gc)
    return (jnp.asarray(cs, dtype=BF16), jnp.asarray(np.ascontiguousarray(tw), dtype=F32),
            jnp.asarray(wch, dtype=BF16))


def _fourier_kernel(x_ref, cs_ref, tw_ref, wch_ref, o_ref, *, gc, n_groups, tk):
    n1 = x_ref.shape[0]
    d_f = gc * n_groups
    cs = cs_ref[...]
    wch = wch_ref[...]
    reps = gc // 128 if gc >= 128 else 1

    def lanes(t):
        if gc < 128:
            return t[:, :gc]
        return jnp.concatenate([t] * reps, axis=1) if reps > 1 else t

    c1, s1, c2, s2, c3, s3 = [lanes(tw_ref[i]) for i in range(2 * (RADIX - 1))]
    for g in range(n_groups):
        zc, zs = [], []
        for r in range(RADIX):
            z = jnp.dot(cs, x_ref[:, r * d_f + g * gc:r * d_f + (g + 1) * gc],
                        preferred_element_type=F32)
            zc.append(z[:tk])
            zs.append(z[tk:])
        t0r, t0i = zc[0], -zs[0]
        t1r, t1i = c1 * zc[1] - s1 * zs[1], -(c1 * zs[1] + s1 * zc[1])
        t2r, t2i = c2 * zc[2] - s2 * zs[2], -(c2 * zs[2] + s2 * zc[2])
        t3r, t3i = c3 * zc[3] - s3 * zs[3], -(c3 * zs[3] + s3 * zc[3])
        ar, ai = t0r + t2r, t0i + t2i
        br, bi = t0r - t2r, t0i - t2i
        cr, ci = t1r + t3r, t1i + t3i
        dr, di = t1r - t3r, t1i - t3i
        xq = [(ar + cr, ai + ci), (br + di, bi - dr), (ar - cr, ai - ci), (br - di, bi + dr)]
        for q in range(RADIX):
            xre, xim = xq[q]
            lhs = jnp.concatenate([xre.astype(BF16), xim.astype(BF16)], axis=1)
            y = jnp.dot(lhs, wch, preferred_element_type=F32)
            o_ref[q, :, g * gc:(g + 1) * gc] = y.astype(BF16)


def _fourier(uf, B, S):
    d_f = uf.shape[-1]
    gc = d_f // N_FOURIER_GROUPS
    n1 = S // RADIX
    tk = min(512, n1)
    cs, tw, wch = _dft_constants(S, gc, tk)
    x4 = uf.reshape(B, n1, RADIX * d_f)
    kern = functools.partial(_fourier_kernel, gc=gc, n_groups=N_FOURIER_GROUPS, tk=tk)
    out = pl.pallas_call(
        kern,
        grid=(B, n1 // tk),
        in_specs=[pl.BlockSpec((None, n1, RADIX * d_f), lambda b, t: (b, 0, 0)),
                  pl.BlockSpec((None, 2 * tk, n1), lambda b, t: (t, 0, 0)),
                  pl.BlockSpec((2 * (RADIX - 1), tk, 128), lambda b, t: (0, t, 0)),
                  pl.BlockSpec((2 * gc, gc), lambda b, t: (0, 0))],
        out_specs=pl.BlockSpec((None, RADIX, tk, d_f), lambda b, t: (b, 0, t, 0)),
        out_shape=jax.ShapeDtypeStruct((B, RADIX, n1, d_f), BF16),
        compiler_params=_params(("arbitrary", "arbitrary")),
        name="fourier",
    )(x4, cs, tw, wch)
    return out.reshape(B * S, d_f)


_COL_A, _COL_SINT, _COL_EML, _COL_WK, _COL_SOLD = range(5)
N_COLS = 8


def _lane_scan(x, op, identity):
    n = x.shape[-1]
    lane = lax.broadcasted_iota(jnp.int32, x.shape, x.ndim - 1)
    shift = 1
    while shift < n:
        rolled = pltpu.roll(x, shift, axis=x.ndim - 1)
        x = op(x, jnp.where(lane >= shift, rolled, identity))
        shift *= 2
    return x


def _gateprep_kernel(ig_ref, fg_ref, cols_ref, r_ref):
    nc, rows, L = ig_ref.shape
    m = jnp.full((rows, 1), M_INIT, F32)
    for c in range(nc):
        ig = ig_ref[c]
        lf = jax.nn.log_sigmoid(fg_ref[c])
        b = _lane_scan(lf, jnp.add, 0.0)
        g = b[:, L - 1:L]
        r = ig - b
        cm = _lane_scan(r, jnp.maximum, -jnp.inf)
        m_loc = g + jnp.max(r, axis=-1, keepdims=True)
        m_new = jnp.maximum(g + m, m_loc)
        top = jnp.maximum(cm, m)
        r_ref[c] = r
        cols_ref[_COL_A, c] = -top
        cols_ref[_COL_SINT, c] = jnp.exp(m - top)
        cols_ref[_COL_EML, c] = jnp.exp(-(b + top))
        cols_ref[_COL_WK, c] = jnp.exp(g + r - m_new)
        cols_ref[_COL_SOLD, c] = jnp.broadcast_to(jnp.exp(g + m - m_new), (rows, L))
        m = m_new


def _gateprep(gates, B, S):
    H, L = N_HEADS, min(MLSTM_CHUNK, S)
    nc = S // L
    g4 = gates[:, :4 * H].reshape(B, S, 4, H)
    ig = jnp.stack([g4[:, :, 0], jnp.flip(g4[:, :, 2], axis=1)])
    fg = jnp.stack([g4[:, :, 1], jnp.flip(g4[:, :, 3], axis=1)])
    to_rows = lambda a: a.transpose(2, 0, 1, 3).reshape(nc, L, 2 * B * H).transpose(0, 2, 1)
    nrow = 2 * B * H
    cols, r = pl.pallas_call(
        _gateprep_kernel,
        out_shape=[jax.ShapeDtypeStruct((5, nc, nrow, L), F32),
                   jax.ShapeDtypeStruct((nc, nrow, L), F32)],
        name="gateprep",
    )(to_rows(ig), to_rows(fg))
    def unrows(a):
        lead = a.shape[:-3]
        a = jnp.moveaxis(a, -2, -3).reshape(lead + (2, B, H, S))
        return jnp.concatenate([a[..., 0:1, :, :, :], jnp.flip(a[..., 1:2, :, :, :], axis=-1)],
                               axis=-4)
    cols = unrows(cols)
    r = unrows(r)
    cols = jnp.pad(cols, ((0, N_COLS - 5), (0, 0), (0, 0), (0, 0), (0, 0)))
    cols = cols.transpose(2, 3, 4, 1, 0).reshape(B, H, S, 2 * N_COLS)
    rows = jnp.pad(r.transpose(1, 2, 0, 3), ((0, 0), (0, 0), (0, 6), (0, 0)))
    return cols, rows


def _mlstm_kernel(q_ref, k_ref, v_ref, og_ref, cols_ref, rows_ref, nw_ref, o_ref, hf_ref, *, L):
    S, dk = q_ref.shape
    dv = v_ref.shape[1]
    nc = S // L
    jj = lax.broadcasted_iota(jnp.int32, (L, L), 0)
    tt = lax.broadcasted_iota(jnp.int32, (L, L), 1)

    def chunk(c, carry, direction):
        C, n = carry
        sl = pl.ds(pl.multiple_of(c * L, L), L)
        qc, kc, vc = q_ref[sl, :], k_ref[sl, :], v_ref[sl, :]
        cb = direction * N_COLS
        col = lambda i: cols_ref[sl, cb + i:cb + i + 1]
        r = rows_ref[direction:direction + 1, sl]
        allowed = (tt <= jj) if direction == 0 else (tt >= jj)
        s = lax.dot_general(qc, kc, (((1,), (1,)), ((), ())), preferred_element_type=F32)
        p = jnp.where(allowed, jnp.exp(col(_COL_A) + r), 0.0) * s
        sint = col(_COL_SINT)
        inter = jnp.dot(qc, C.astype(BF16), preferred_element_type=F32)
        num = jnp.dot(p.astype(BF16), vc, preferred_element_type=F32) + sint * inter
        qn = jnp.sum(qc.astype(F32) * n, axis=-1, keepdims=True)
        den = jnp.sum(p, axis=-1, keepdims=True) + sint * qn
        h = num / jnp.maximum(jnp.abs(den), col(_COL_EML))
        kw = kc.astype(F32) * col(_COL_WK)
        sold = cols_ref[pl.ds(c * L, 1), cb + _COL_SOLD:cb + _COL_SOLD + 1]
        upd = lax.dot_general(kw.astype(BF16), vc, (((0,), (0,)), ((), ())),
                              preferred_element_type=F32)
        C = sold * C + upd
        n = sold * n + jnp.sum(kw, axis=0, keepdims=True)
        return sl, h, (C, n)

    init = (jnp.zeros((dk, dv), F32), jnp.zeros((1, dk), F32))

    def fwd(c, carry):
        sl, h, carry = chunk(c, carry, 0)
        hf_ref[sl, :] = h
        return carry

    lax.fori_loop(0, nc, fwd, init)

    def bwd(i, carry):
        sl, h, carry = chunk(nc - 1 - i, carry, 1)
        h = h + hf_ref[sl, :]
        y = h * lax.rsqrt(jnp.mean(h * h, axis=-1, keepdims=True) + EPS)
        o_ref[sl, :] = (y * nw_ref[...] * og_ref[sl, :].astype(F32)).astype(BF16)
        return carry

    lax.fori_loop(0, nc, bwd, init)


def _mlstm(q, k, v, og, cols, rows, norm_w, B, S):
    H = N_HEADS
    dk = q.shape[-1] // H
    dv = v.shape[-1] // H
    L = min(MLSTM_CHUNK, S)
    seq = lambda width: pl.BlockSpec((None, S, width), lambda b, h: (b, 0, h))
    kern = functools.partial(_mlstm_kernel, L=L)
    out = pl.pallas_call(
        kern,
        grid=(B, H),
        in_specs=[seq(dk), seq(dk), seq(dv), seq(dv),
                  pl.BlockSpec((None, None, S, 2 * N_COLS), lambda b, h: (b, h, 0, 0)),
                  pl.BlockSpec((None, None, 8, S), lambda b, h: (b, h, 0, 0)),
                  pl.BlockSpec((1, dv), lambda b, h: (0, h))],
        out_specs=seq(dv),
        out_shape=jax.ShapeDtypeStruct((B, S, H * dv), BF16),
        scratch_shapes=[pltpu.VMEM((S, dv), F32)],
        compiler_params=_params(("arbitrary", "arbitrary")),
        name="mlstm",
    )(q.reshape(B, S, H * dk), k.reshape(B, S, H * dk), v.reshape(B, S, H * dv),
      og.reshape(B, S, H * dv), cols, rows, norm_w.reshape(1, H * dv))
    return out.reshape(B * S, H * dv)


def _outproj_kernel(yf_ref, ym_ref, x_ref, mod_ref, nw_ref, w_ref, x1_ref, h2_ref):
    d_f = yf_ref.shape[1]
    proj = jnp.dot(yf_ref[...], w_ref[0:d_f, :], preferred_element_type=F32)
    proj = proj + jnp.dot(ym_ref[...], w_ref[d_f:, :], preferred_element_type=F32)
    x1 = x_ref[...] + mod_ref[2:3, :] * proj
    x1_ref[...] = x1
    h2_ref[...] = _rms_mod(x1, nw_ref[...], mod_ref[3:4, :], mod_ref[4:5, :]).astype(BF16)


def _outproj(yf, ym, x2, mod, norm2_w, w_out, S):
    T, D = x2.shape
    d_f, d_m = yf.shape[1], ym.shape[1]
    tm = min(512, S)
    tiles_per_seq = S // tm
    tok = lambda width: pl.BlockSpec((tm, width), lambda i: (i, 0))
    return pl.pallas_call(
        _outproj_kernel,
        grid=(T // tm,),
        in_specs=[tok(d_f), tok(d_m), tok(D),
                  pl.BlockSpec((None, N_MOD, D), lambda i: (i // tiles_per_seq, 0, 0)),
                  pl.BlockSpec((1, D), lambda i: (0, 0)),
                  _resident((d_f + d_m, D), lambda i: (0, 0))],
        out_specs=[tok(D), tok(D)],
        out_shape=[jax.ShapeDtypeStruct((T, D), F32), jax.ShapeDtypeStruct((T, D), BF16)],
        compiler_params=_params(("arbitrary",)),
        name="outproj",
    )(yf, ym, x2, mod, norm2_w.reshape(1, D), w_out)


def _ffn_kernel(hp_ref, hc_ref, hn_ref, x1_ref, mod_ref, nfw_ref,
                wa_ref, wv_ref, ca_ref, cv_ref, ba_ref, bv_ref, wd_ref,
                o_ref, acc_ref, *, tiles_per_seq):
    i, j = pl.program_id(0), pl.program_id(1)
    tm = hc_ref.shape[0]
    s_idx = i % tiles_per_seq
    prev = jnp.where(s_idx > 0, hp_ref[...], jnp.zeros_like(hp_ref))
    nxt = jnp.where(s_idx < tiles_per_seq - 1, hn_ref[...], jnp.zeros_like(hn_ref))
    hext = jnp.concatenate([prev, hc_ref[...], nxt], axis=0)

    def branch(w_ref, cw_ref, b_ref):
        u = jnp.dot(hext, w_ref[...], preferred_element_type=F32)
        n = u.shape[0]
        up = pltpu.roll(u, 1, axis=0)[HALO:HALO + tm]
        dn = pltpu.roll(u, n - 1, axis=0)[HALO:HALO + tm]
        return (cw_ref[0:1, :] * up + cw_ref[1:2, :] * u[HALO:HALO + tm]
                + cw_ref[2:3, :] * dn + b_ref[...])

    a = branch(wa_ref, ca_ref, ba_ref)
    v = branch(wv_ref, cv_ref, bv_ref)
    gact = (a * jax.nn.sigmoid(a) * v).astype(BF16)
    part = jnp.dot(gact, wd_ref[...], preferred_element_type=F32)

    @pl.when(j == 0)
    def _():
        acc_ref[...] = part

    @pl.when(j > 0)
    def _():
        acc_ref[...] += part

    @pl.when(j == pl.num_programs(1) - 1)
    def _():
        y = x1_ref[...] + mod_ref[5:6, :] * acc_ref[...]
        ms = jnp.mean(y * y, axis=-1, keepdims=True)
        o_ref[...] = y * lax.rsqrt(ms + EPS) * nfw_ref[...]


def _ffn(h2, x1, mod, norm_f_w, w_up, w_conv, b_conv, w_down, S):
    T, D = x1.shape
    d_ff = w_down.shape[0]
    tm = min(512, S)
    tn = min(512, d_ff)
    tiles_per_seq = S // tm
    nj = d_ff // tn
    hb = tm // HALO
    last = T // HALO - 1
    kern = functools.partial(_ffn_kernel, tiles_per_seq=tiles_per_seq)
    return pl.pallas_call(
        kern,
        grid=(T // tm, nj),
        in_specs=[pl.BlockSpec((HALO, D), lambda i, j: (jnp.maximum(i * hb - 1, 0), 0)),
                  pl.BlockSpec((tm, D), lambda i, j: (i, 0)),
                  pl.BlockSpec((HALO, D), lambda i, j: (jnp.minimum((i + 1) * hb, last), 0)),
                  pl.BlockSpec((tm, D), lambda i, j: (i, 0)),
                  pl.BlockSpec((None, N_MOD, D), lambda i, j: (i // tiles_per_seq, 0, 0)),
                  pl.BlockSpec((1, D), lambda i, j: (0, 0)),
                  pl.BlockSpec((D, tn), lambda i, j: (0, j)),
                  pl.BlockSpec((D, tn), lambda i, j: (0, j + nj)),
                  pl.BlockSpec((3, tn), lambda i, j: (0, j)),
                  pl.BlockSpec((3, tn), lambda i, j: (0, j + nj)),
                  pl.BlockSpec((1, tn), lambda i, j: (0, j)),
                  pl.BlockSpec((1, tn), lambda i, j: (0, j + nj)),
                  pl.BlockSpec((tn, D), lambda i, j: (j, 0))],
        out_specs=pl.BlockSpec((tm, D), lambda i, j: (i, 0)),
        out_shape=jax.ShapeDtypeStruct((T, D), F32),
        scratch_shapes=[pltpu.VMEM((tm, D), F32)],
        compiler_params=_params(("arbitrary", "arbitrary")),
        name="ffn",
    )(h2, h2, h2, x1, mod, norm_f_w.reshape(1, D), w_up, w_up, w_conv, w_conv,
      b_conv.reshape(1, -1), b_conv.reshape(1, -1), w_down)


def kernel(x, c, w_ada, b_ada, norm1_w, w_in, b_gates, mlstm_norm_w, w_out, norm2_w, w_up,
           w_conv, b_conv, w_down, norm_f_w):
    B, S, D = x.shape
    d_f = D // 2
    d_m = D - d_f
    d_qk = d_m // 2
    dims = (d_f, d_qk, d_m)
    n_main = d_f + 2 * d_qk + 2 * d_m
    n_gates = 4 * N_HEADS
    assert w_in.shape[1] == n_main + n_gates and S % RADIX == 0

    x2 = x.reshape(B * S, D)
    w_main = w_in[:, :n_main].astype(BF16)
    w_gate = jnp.pad(w_in[:, n_main:], ((0, 0), (0, N_GATE_LANES - n_gates))).astype(BF16)
    b_gate = jnp.pad(b_gates, (0, N_GATE_LANES - n_gates)).reshape(1, N_GATE_LANES)

    mod = _adaln(c, w_ada, b_ada)
    uf, q, k, v, og, gates = _inproj(x2, mod, norm1_w, w_main, w_gate, b_gate, S, dims)
    yf = _fourier(uf, B, S)
    cols, rows = _gateprep(gates, B, S)
    ym = _mlstm(q, k, v, og, cols, rows, mlstm_norm_w, B, S)
    x1, h2 = _outproj(yf, ym, x2, mod, norm2_w, w_out.astype(BF16), S)
    out = _ffn(h2, x1, mod, norm_f_w, w_up.astype(BF16), w_conv, b_conv,
               w_down.astype(BF16), S)
    return out.reshape(B, S, D)
```

```python
import functools

import numpy as np
import jax
import jax.numpy as jnp
from jax import lax
from jax.experimental import pallas as pl
from jax.experimental.pallas import tpu as pltpu

F32 = jnp.float32
BF16 = jnp.bfloat16

N_MOD = 6
N_FOURIER_GROUPS = 4
N_HEADS = 4
RADIX = 4
EPS = 1e-6
M_INIT = -1e30
MLSTM_CHUNK = 128
N_GATE_LANES = 128
HALO = 16
V7X_VMEM_LIMIT = 56 * 1024 * 1024


def _params(sem, vmem=V7X_VMEM_LIMIT):
    return pltpu.CompilerParams(dimension_semantics=sem, vmem_limit_bytes=vmem)


def _resident(block_shape, index_map):
    return pl.BlockSpec(block_shape, index_map, pipeline_mode=pl.Buffered(1))


def _adaln_kernel(c_ref, w_ref, b_ref, o_ref):
    c = c_ref[...]
    s = c * jax.nn.sigmoid(c)
    s_hi = s.astype(BF16).astype(F32)
    row = lax.broadcasted_iota(jnp.int32, s.shape, 0)
    lhs = jnp.where(row < 8, s_hi, s - s_hi).astype(BF16)
    w = w_ref[...]
    w_hi = w.astype(BF16)
    w_lo = (w - w_hi.astype(F32)).astype(BF16)
    r = jnp.dot(lhs, w_hi, preferred_element_type=F32)
    r = r + jnp.dot(lhs, w_lo, preferred_element_type=F32)
    o_ref[...] = r[0:8] + r[8:16] + b_ref[...]


def _adaln(c, w_ada, b_ada):
    B, D = c.shape
    N = w_ada.shape[1]
    tn = min(1024, D)
    cp =jnp.zeros((16, D), F32).at[0:B].set(c).at[8:8 + B].set(c)
    out = pl.pallas_call(
        _adaln_kernel,
        grid=(N // tn,),
        in_specs=[pl.BlockSpec((16, D), lambda j: (0, 0)),
                  pl.BlockSpec((D, tn), lambda j: (0, j)),
                  pl.BlockSpec((1, tn), lambda j: (0, j))],
        out_specs=pl.BlockSpec((8, tn), lambda j: (0, j)),
        out_shape=jax.ShapeDtypeStruct((8, N), F32),
        compiler_params=_params(("arbitrary",)),
        name="adaln",
    )(cp, w_ada, b_ada.reshape(1, N))
    return out[:B].reshape(B, N_MOD, D)


def _cast_in_kernel(w_ref, main_ref, gate_ref, *, n_main_blocks, n_gates):
    j = pl.program_id(0)

    @pl.when(j < n_main_blocks)
    def _():
        main_ref[...] = w_ref[...].astype(BF16)

    @pl.when(j == n_main_blocks)
    def _():
        w = w_ref[0:N_GATE_LANES, :]
        row = lax.broadcasted_iota(jnp.int32, w.shape, 0)
        gate_ref[...] = jnp.where(row < n_gates, w, 0.0).astype(BF16)


def _cast_w_in(w_in_t, n_main, n_gates):
    D = w_in_t.shape[1]
    rb = min(512, n_main)
    nb = n_main // rb
    kern = functools.partial(_cast_in_kernel, n_main_blocks=nb, n_gates=n_gates)
    return pl.pallas_call(
        kern,
        grid=(nb + 1,),
        in_specs=[pl.BlockSpec((rb, D), lambda j: (j, 0))],
        out_specs=[pl.BlockSpec((rb, D), lambda j: (jnp.minimum(j, nb - 1), 0)),
                   pl.BlockSpec((N_GATE_LANES, D), lambda j: (0, 0))],
        out_shape=[jax.ShapeDtypeStruct((n_main, D), BF16),
                   jax.ShapeDtypeStruct((N_GATE_LANES, D), BF16)],
        compiler_params=_params(("arbitrary",)),
        name="cast_w_in",
    )(w_in_t)


def _rms_mod(x, nw, shift, scale):
    ms = jnp.mean(x * x, axis=-1, keepdims=True)
    return x * lax.rsqrt(ms + EPS) * (nw * (1.0 + scale)) + shift


def _inproj_kernel(x_ref, mod_ref, nw_ref, w_ref, wg_ref, bg_ref,
                   uf_ref, q_ref, k_ref, v_ref, og_ref, g_ref, uf_scr, *, dims, qscale):
    d_f, d_qk, d_m = dims
    tm = x_ref.shape[0]
    h = _rms_mod(x_ref[...], nw_ref[...], mod_ref[0:1, :], mod_ref[1:2, :]).astype(BF16)

    nt = (((1,), (1,)), ((), ()))

    def proj(c0, width):
        return lax.dot_general(h, w_ref[c0:c0 + width, :], nt, preferred_element_type=F32)

    cw = min(512, d_qk)
    lw = uf_scr.shape[-1]
    for c0 in range(0, d_f, cw):
        u = proj(c0, cw)
        for l in range(cw // lw):
            uf_scr[l] = u[:, l * lw:(l + 1) * lw]
        for l in range(cw // lw):
            for r in range(RADIX):
                col = r * d_f + c0 + l * lw
                uf_ref[:, col:col + lw] = (
                    uf_scr[l, pl.ds(r, tm // RADIX, stride=RADIX), :].astype(BF16))
    o = d_f
    for c0 in range(0, d_qk, cw):
        q_ref[:, c0:c0 + cw] = (proj(o + c0, cw) * qscale).astype(BF16)
    o += d_qk
    for c0 in range(0, d_qk, cw):
        k_ref[:, c0:c0 + cw] = proj(o + c0, cw).astype(BF16)
    o += d_qk
    for c0 in range(0, d_m, cw):
        v_ref[:, c0:c0 + cw] = proj(o + c0, cw).astype(BF16)
    o += d_m
    for c0 in range(0, d_m, cw):
        og_ref[:, c0:c0 + cw] = jax.nn.sigmoid(proj(o + c0, cw)).astype(BF16)
    g_ref[...] = lax.dot_general(h, wg_ref[...], nt, preferred_element_type=F32) + bg_ref[...]


def _inproj(x2, mod, norm1_w, w_main, w_gate, b_gate, S, dims):
    T, D = x2.shape
    d_f, d_qk, d_m = dims
    tm = min(512, S)
    tiles_per_seq = S // tm
    n_main = w_main.shape[0]
    tok = lambda width: pl.BlockSpec((tm, width), lambda i: (i, 0))
    kern = functools.partial(_inproj_kernel, dims=dims,
                             qscale=float((d_qk // N_HEADS) ** -0.5))
    return pl.pallas_call(
        kern,
        grid=(T // tm,),
        in_specs=[tok(D),
                  pl.BlockSpec((None, N_MOD, D), lambda i: (i // tiles_per_seq, 0, 0)),
                  pl.BlockSpec((1, D), lambda i: (0, 0)),
                  _resident((n_main, D), lambda i: (0, 0)),
                  _resident((N_GATE_LANES, D), lambda i: (0, 0)),
                  pl.BlockSpec((1, N_GATE_LANES), lambda i: (0, 0))],
        out_specs=[pl.BlockSpec((tm // RADIX, RADIX * d_f), lambda i: (i, 0)),
                   tok(d_qk), tok(d_qk), tok(d_m), tok(d_m), tok(N_GATE_LANES)],
        out_shape=[jax.ShapeDtypeStruct((T // RADIX, RADIX * d_f), BF16),
                   jax.ShapeDtypeStruct((T, d_qk), BF16),
                   jax.ShapeDtypeStruct((T, d_qk), BF16),
                   jax.ShapeDtypeStruct((T, d_m), BF16),
                   jax.ShapeDtypeStruct((T, d_m), BF16),
                   jax.ShapeDtypeStruct((T, N_GATE_LANES), F32)],
        scratch_shapes=[pltpu.VMEM((max(1, min(512, d_qk) // 128), tm, min(128, d_qk)), F32)],
        compiler_params=_params(("arbitrary",)),
        name="inproj",
    )(x2, mod, norm1_w.reshape(1, D), w_main, w_gate, b_gate)


def _dft_constants(S, gc, tk):
    n1 = S // RADIX
    idx = np.arange(n1, dtype=np.int64)
    ang = 2.0 * np.pi * ((idx[:, None] * idx[None, :]) % n1).astype(np.float64) / n1
    c, s = np.cos(ang), np.sin(ang)
    nt = n1 // tk
    cs = np.concatenate([c.reshape(nt, tk, n1), s.reshape(nt, tk, n1)], axis=1)
    tw = []
    for r in range(1, RADIX):
        a = 2.0 * np.pi * ((r * idx) % S).astype(np.float64) / S
        tw += [np.cos(a), np.sin(a)]
    tw = np.broadcast_to(np.stack(tw)[:, :, None], (2 * (RADIX - 1), n1, 128))
    ch = np.arange(gc, dtype=np.int64)
    angc = 2.0 * np.pi * ((ch[:, None] * ch[None, :]) % gc).astype(np.float64) / gc
    wch = np.concatenate([np.cos(angc), np.sin(angc)], axis=0) / np.sqrt(float(S) * gc)
    return (jnp.asarray(cs, dtype=BF16), jnp.asarray(np.ascontiguousarray(tw), dtype=F32),
            jnp.asarray(wch, dtype=BF16))


def _fourier_kernel(x_ref, cs_ref, tw_ref, wch_ref, o_ref, *, gc, n_groups, tk):
    n1 = x_ref.shape[0]
    d_f = gc * n_groups
    cs = cs_ref[...]
    wch = wch_ref[...]
    reps = gc // 128 if gc >= 128 else 1

    def lanes(t):
        if gc < 128:
            return t[:, :gc]
        return jnp.concatenate([t] * reps, axis=1) if reps > 1 else t

    c1, s1, c2, s2, c3, s3 = [lanes(tw_ref[i]) for i in range(2 * (RADIX - 1))]
    for g in range(n_groups):
        zc, zs = [], []
        for r in range(RADIX):
            z = jnp.dot(cs, x_ref[:, r * d_f + g * gc:r * d_f + (g + 1) * gc],
                        preferred_element_type=F32)
            zc.append(z[:tk])
            zs.append(z[tk:])
        t0r, t0i = zc[0], -zs[0]
        t1r, t1i = c1 * zc[1] - s1 * zs[1], -(c1 * zs[1] + s1 * zc[1])
        t2r, t2i = c2 * zc[2] - s2 * zs[2], -(c2 * zs[2] + s2 * zc[2])
        t3r, t3i = c3 * zc[3] - s3 * zs[3], -(c3 * zs[3] + s3 * zc[3])
        ar, ai = t0r + t2r, t0i + t2i
        br, bi = t0r - t2r, t0i - t2i
        cr, ci = t1r + t3r, t1i + t3i
        dr, di = t1r - t3r, t1i - t3i
        xq = [(ar + cr, ai + ci), (br + di, bi - dr), (ar - cr, ai - ci), (br - di, bi + dr)]
        for q in range(RADIX):
            xre, xim = xq[q]
            lhs = jnp.concatenate([xre.astype(BF16), xim.astype(BF16)], axis=1)
            y = jnp.dot(lhs, wch, preferred_element_type=F32)
            o_ref[q, :, g * gc:(g + 1) * gc] = y.astype(BF16)


def _fourier(uf4, B, S):
    d_f = uf4.shape[-1] // RADIX
    gc = d_f // N_FOURIER_GROUPS
    n1 = S // RADIX
    tk = min(512, n1)
    cs, tw, wch = _dft_constants(S, gc, tk)
    x4 = uf4.reshape(B, n1, RADIX * d_f)
    kern = functools.partial(_fourier_kernel, gc=gc, n_groups=N_FOURIER_GROUPS, tk=tk)
    out = pl.pallas_call(
        kern,
        grid=(B, n1 // tk),
        in_specs=[pl.BlockSpec((None, n1, RADIX * d_f), lambda b, t: (b, 0, 0)),
                  pl.BlockSpec((None, 2 * tk, n1), lambda b, t: (t, 0, 0)),
                  pl.BlockSpec((2 * (RADIX - 1), tk, 128), lambda b, t: (0, t, 0)),
                  pl.BlockSpec((2 * gc, gc), lambda b, t: (0, 0))],
        out_specs=pl.BlockSpec((None, RADIX, tk, d_f), lambda b, t: (b, 0, t, 0)),
        out_shape=jax.ShapeDtypeStruct((B, RADIX, n1, d_f), BF16),
        compiler_params=_params(("arbitrary", "arbitrary")),
        name="fourier",
    )(x4, cs, tw, wch)
    return out.reshape(B * S, d_f)


_COL_A, _COL_SINT, _COL_EML, _COL_WK, _COL_SOLD = range(5)
N_COLS = 8


def _lane_scan(x, op, identity):
    n = x.shape[-1]
    lane = lax.broadcasted_iota(jnp.int32, x.shape, x.ndim - 1)
    shift = 1
    while shift < n:
        rolled = pltpu.roll(x, shift, axis=x.ndim - 1)
        x = op(x, jnp.where(lane >= shift, rolled, identity))
        shift *= 2
    return x


def _gateprep_kernel(ig_ref, fg_ref, cols_ref, r_ref):
    nc, rows, L = ig_ref.shape
    m = jnp.full((rows, 1), M_INIT, F32)
    for c in range(nc):
        ig = ig_ref[c]
        lf = jax.nn.log_sigmoid(fg_ref[c])
        b = _lane_scan(lf, jnp.add, 0.0)
        g = b[:, L - 1:L]
        r = ig - b
        cm = _lane_scan(r, jnp.maximum, -jnp.inf)
        m_loc = g + jnp.max(r, axis=-1, keepdims=True)
        m_new = jnp.maximum(g + m, m_loc)
        top = jnp.maximum(cm, m)
        r_ref[c] = r
        cols_ref[_COL_A, c] = -top
        cols_ref[_COL_SINT, c] = jnp.exp(m - top)
        cols_ref[_COL_EML, c] = jnp.exp(-(b + top))
        cols_ref[_COL_WK, c] = jnp.exp(g + r - m_new)
        cols_ref[_COL_SOLD, c] = jnp.broadcast_to(jnp.exp(g + m - m_new), (rows, L))
        m = m_new


def _gateprep(gates, B, S):
    H, L = N_HEADS, min(MLSTM_CHUNK, S)
    nc = S // L
    g4 = gates[:, :4 * H].reshape(B, S, 4, H)
    ig = jnp.stack([g4[:, :, 0], jnp.flip(g4[:, :, 2], axis=1)])
    fg = jnp.stack([g4[:, :, 1], jnp.flip(g4[:, :, 3], axis=1)])
    to_rows = lambda a: a.transpose(2, 0, 1, 3).reshape(nc, L, 2 * B * H).transpose(0, 2, 1)
    nrow = 2 * B * H
    cols, r = pl.pallas_call(
        _gateprep_kernel,
        out_shape=[jax.ShapeDtypeStruct((5, nc, nrow, L), F32),
                   jax.ShapeDtypeStruct((nc, nrow, L), F32)],
        name="gateprep",
    )(to_rows(ig), to_rows(fg))
    def unrows(a):
        lead = a.shape[:-3]
        a = jnp.moveaxis(a, -2, -3).reshape(lead + (2, B, H, S))
        return jnp.concatenate([a[..., 0:1, :, :, :], jnp.flip(a[..., 1:2, :, :, :], axis=-1)],
                               axis=-4)
    cols = unrows(cols)
    r = unrows(r)
    cols = jnp.pad(cols, ((0, N_COLS - 5), (0, 0), (0, 0), (0, 0), (0, 0)))
    cols = cols.transpose(2, 3, 4, 1, 0).reshape(B, H, S, 2 * N_COLS)
    rows = jnp.pad(r.transpose(1, 2, 0, 3), ((0, 0), (0, 0), (0, 6), (0, 0)))
    return cols, rows


def _mlstm_kernel(q_ref, k_ref, v_ref, og_ref, cols_ref, rows_ref, nw_ref, o_ref, hf_ref, *, L):
    S, dk = q_ref.shape
    dv = v_ref.shape[1]
    nc = S // L
    jj = lax.broadcasted_iota(jnp.int32, (L, L), 0)
    tt = lax.broadcasted_iota(jnp.int32, (L, L), 1)

    def chunk(c, carry, direction):
        C, n = carry
        sl = pl.ds(pl.multiple_of(c * L, L), L)
        qc, kc, vc = q_ref[sl, :], k_ref[sl, :], v_ref[sl, :]
        cb = direction * N_COLS
        col = lambda i: cols_ref[sl, cb + i:cb + i + 1]
        r = rows_ref[direction:direction + 1, sl]
        allowed = (tt <= jj) if direction == 0 else (tt >= jj)
        s = lax.dot_general(qc, kc, (((1,), (1,)), ((), ())), preferred_element_type=F32)
        p = jnp.where(allowed, jnp.exp(col(_COL_A) + r), 0.0) * s
        sint = col(_COL_SINT)
        inter = jnp.dot(qc, C.astype(BF16), preferred_element_type=F32)
        num = jnp.dot(p.astype(BF16), vc, preferred_element_type=F32) + sint * inter
        qn = jnp.sum(qc.astype(F32) * n, axis=-1, keepdims=True)
        den = jnp.sum(p, axis=-1, keepdims=True) + sint * qn
        h = num / jnp.maximum(jnp.abs(den), col(_COL_EML))
        kw = kc.astype(F32) * col(_COL_WK)
        sold = cols_ref[pl.ds(c * L, 1), cb + _COL_SOLD:cb + _COL_SOLD + 1]
        upd = lax.dot_general(kw.astype(BF16), vc, (((0,), (0,)), ((), ())),
                              preferred_element_type=F32)
        C = sold * C + upd
        n = sold * n + jnp.sum(kw, axis=0, keepdims=True)
        return sl, h, (C, n)

    init = (jnp.zeros((dk, dv), F32), jnp.zeros((1, dk), F32))

    def fwd(c, carry):
        sl, h, carry = chunk(c, carry, 0)
        hf_ref[sl, :] = h
        return carry

    lax.fori_loop(0, nc, fwd, init)

    def bwd(i, carry):
        sl, h, carry = chunk(nc - 1 - i, carry, 1)
        h = h + hf_ref[sl, :]
        y = h * lax.rsqrt(jnp.mean(h * h, axis=-1, keepdims=True) + EPS)
        o_ref[sl, :] = (y * nw_ref[...] * og_ref[sl, :].astype(F32)).astype(BF16)
        return carry

    lax.fori_loop(0, nc, bwd, init)


def _mlstm(q, k, v, og, cols, rows, norm_w, B, S):
    H = N_HEADS
    dk = q.shape[-1] // H
    dv = v.shape[-1] // H
    L = min(MLSTM_CHUNK, S)
    seq = lambda width: pl.BlockSpec((None, S, width), lambda b, h: (b, 0, h))
    kern = functools.partial(_mlstm_kernel, L=L)
    out = pl.pallas_call(
        kern,
        grid=(B, H),
        in_specs=[seq(dk), seq(dk), seq(dv), seq(dv),
                  pl.BlockSpec((None, None, S, 2 * N_COLS), lambda b, h: (b, h, 0, 0)),
                  pl.BlockSpec((None, None, 8, S), lambda b, h: (b, h, 0, 0)),
                  pl.BlockSpec((1, dv), lambda b, h: (0, h))],
        out_specs=seq(dv),
        out_shape=jax.ShapeDtypeStruct((B, S, H * dv), BF16),
        scratch_shapes=[pltpu.VMEM((S, dv), F32)],
        compiler_params=_params(("arbitrary", "arbitrary")),
        name="mlstm",
    )(q.reshape(B, S, H * dk), k.reshape(B, S, H * dk), v.reshape(B, S, H * dv),
      og.reshape(B, S, H * dv), cols, rows, norm_w.reshape(1, H * dv))
    return out.reshape(B * S, H * dv)


def _outproj_kernel(yf_ref, ym_ref, x_ref, mod_ref, nw_ref, w_ref, x1_ref, h2_ref):
    d_f = yf_ref.shape[1]
    proj = jnp.dot(yf_ref[...], w_ref[0:d_f, :], preferred_element_type=F32)
    proj = proj + jnp.dot(ym_ref[...], w_ref[d_f:, :], preferred_element_type=F32)
    x1 = x_ref[...] + mod_ref[2:3, :] * proj
    x1_ref[...] = x1
    h2_ref[...] = _rms_mod(x1, nw_ref[...], mod_ref[3:4, :], mod_ref[4:5, :]).astype(BF16)


def _outproj(yf, ym, x2, mod, norm2_w, w_out, S):
    T, D = x2.shape
    d_f, d_m = yf.shape[1], ym.shape[1]
    tm = min(512, S)
    tiles_per_seq = S // tm
    tok = lambda width: pl.BlockSpec((tm, width), lambda i: (i, 0))
    return pl.pallas_call(
        _outproj_kernel,
        grid=(T // tm,),
        in_specs=[tok(d_f), tok(d_m), tok(D),
                  pl.BlockSpec((None, N_MOD, D), lambda i: (i // tiles_per_seq, 0, 0)),
                  pl.BlockSpec((1, D), lambda i: (0, 0)),
                  _resident((d_f + d_m, D), lambda i: (0, 0))],
        out_specs=[tok(D), tok(D)],
        out_shape=[jax.ShapeDtypeStruct((T, D), F32), jax.ShapeDtypeStruct((T, D), BF16)],
        compiler_params=_params(("arbitrary",)),
        name="outproj",
    )(yf, ym, x2, mod, norm2_w.reshape(1, D), w_out)


FFN_COLS = 256
FFN_SUB = 2


def _interleave_gate_value(w, d_ff):
    lead = w.shape[:-1]
    nb = d_ff // FFN_COLS
    w = w.reshape(lead + (2, nb, FFN_COLS))
    return jnp.swapaxes(w, -3, -2).reshape(lead + (2 * d_ff,))


def _cast_up_kernel(a_ref, v_ref, o_ref):
    for s in range(FFN_SUB):
        lo = s * FFN_COLS
        o_ref[:, 2 * lo:2 * lo + FFN_COLS] = a_ref[:, lo:lo + FFN_COLS].astype(BF16)
        o_ref[:, 2 * lo + FFN_COLS:2 * lo + 2 * FFN_COLS] = v_ref[:, lo:lo + FFN_COLS].astype(BF16)


def _cast_w_up(w_up, d_ff):
    D = w_up.shape[0]
    tn = FFN_SUB * FFN_COLS
    nj = d_ff // tn
    return pl.pallas_call(
        _cast_up_kernel,
        grid=(nj,),
        in_specs=[pl.BlockSpec((D, tn), lambda j: (0, j)),
                  pl.BlockSpec((D, tn), lambda j: (0, j + nj))],
        out_specs=pl.BlockSpec((D, 2 * tn), lambda j: (0, j)),
        out_shape=jax.ShapeDtypeStruct((D, 2 * d_ff), BF16),
        compiler_params=_params(("arbitrary",)),
        name="cast_w_up",
    )(w_up, w_up)


def _ffn_kernel(hp_ref, hc_ref, hn_ref, x1_ref, mod_ref, nfw_ref, wu_ref, cw_ref, cb_ref, wd_ref,
                o_ref, acc_ref, u_ref, *, tiles_per_seq):
    i, j = pl.program_id(0), pl.program_id(1)
    tm = hc_ref.shape[0]
    s_idx = i % tiles_per_seq
    prev = jnp.where(s_idx > 0, hp_ref[...], jnp.zeros_like(hp_ref))
    nxt = jnp.where(s_idx < tiles_per_seq - 1, hn_ref[...], jnp.zeros_like(hn_ref))
    hext = jnp.concatenate([prev, hc_ref[...], nxt], axis=0)

    @pl.when(j == 0)
    def _():
        acc_ref[...] = jnp.zeros_like(acc_ref)

    w2 = 2 * FFN_COLS
    for s in range(FFN_SUB):
        u_ref[s] = jnp.dot(hext, wu_ref[:, s * w2:(s + 1) * w2], preferred_element_type=F32)
    for s in range(FFN_SUB):
        tap = lambda k: u_ref[s, pl.ds(HALO - 1 + k, tm), :]
        cw = cw_ref[:, s * w2:(s + 1) * w2]
        y = cw[0:1] * tap(0) + cw[1:2] * tap(1) + cw[2:3] * tap(2) + cb_ref[:, s * w2:(s + 1) * w2]
        a, v = y[:, :FFN_COLS], y[:, FFN_COLS:]
        gact = (a * jax.nn.sigmoid(a) * v).astype(BF16)
        acc_ref[...] += jnp.dot(gact, wd_ref[s * FFN_COLS:(s + 1) * FFN_COLS, :],
                                preferred_element_type=F32)

    @pl.when(j == pl.num_programs(1) - 1)
    def _():
        y = x1_ref[...] + mod_ref[5:6, :] * acc_ref[...]
        ms = jnp.mean(y * y, axis=-1, keepdims=True)
        o_ref[...] = y * lax.rsqrt(ms + EPS) * nfw_ref[...]


def _ffn(h2, x1, mod, norm_f_w, w_up, w_conv, b_conv, w_down, S):
    T, D = x1.shape
    d_ff = w_down.shape[0]
    tm = min(512, S)
    tn = FFN_SUB * FFN_COLS
    tiles_per_seq = S // tm
    hb = tm // HALO
    last = T // HALO - 1
    kern = functools.partial(_ffn_kernel, tiles_per_seq=tiles_per_seq)
    return pl.pallas_call(
        kern,
        grid=(T // tm, d_ff // tn),
        in_specs=[pl.BlockSpec((HALO, D), lambda i, j: (jnp.maximum(i * hb - 1, 0), 0)),
                  pl.BlockSpec((tm, D), lambda i, j: (i, 0)),
                  pl.BlockSpec((HALO, D), lambda i, j: (jnp.minimum((i + 1) * hb, last), 0)),
                  pl.BlockSpec((tm, D), lambda i, j: (i, 0)),
                  pl.BlockSpec((None, N_MOD, D), lambda i, j: (i // tiles_per_seq, 0, 0)),
                  pl.BlockSpec((1, D), lambda i, j: (0, 0)),
                  pl.BlockSpec((D, 2 * tn), lambda i, j: (0, j)),
                  pl.BlockSpec((3, 2 * tn), lambda i, j: (0, j)),
                  pl.BlockSpec((1, 2 * tn), lambda i, j: (0, j)),
                  pl.BlockSpec((tn, D), lambda i, j: (j, 0))],
        out_specs=pl.BlockSpec((tm, D), lambda i, j: (i, 0)),
        out_shape=jax.ShapeDtypeStruct((T, D), F32),
        scratch_shapes=[pltpu.VMEM((tm, D), F32),
                        pltpu.VMEM((FFN_SUB, tm + 2 * HALO, 2 * FFN_COLS), F32)],
        compiler_params=_params(("arbitrary", "arbitrary")),
        name="ffn",
    )(h2, h2, h2, x1, mod, norm_f_w.reshape(1, D), w_up, w_conv, b_conv, w_down)


def kernel(x, c, w_ada, b_ada, norm1_w, w_in, b_gates, mlstm_norm_w, w_out, norm2_w, w_up,
           w_conv, b_conv, w_down, norm_f_w):
    B, S, D = x.shape
    d_f = D // 2
    d_m = D - d_f
    d_qk = d_m // 2
    dims = (d_f, d_qk, d_m)
    n_main = d_f + 2 * d_qk + 2 * d_m
    n_gates = 4 * N_HEADS
    assert w_in.shape[1] == n_main + n_gates and S % RADIX == 0

    x2 = x.reshape(B * S, D)
    w_main, w_gate = _cast_w_in(w_in.T, n_main, n_gates)
    b_gate = jnp.pad(b_gates, (0, N_GATE_LANES - n_gates)).reshape(1, N_GATE_LANES)

    mod = _adaln(c, w_ada, b_ada)
    uf, q, k, v, og, gates = _inproj(x2, mod, norm1_w, w_main, w_gate, b_gate, S, dims)
    yf = _fourier(uf, B, S)
    cols, rows = _gateprep(gates, B, S)
    ym = _mlstm(q, k, v, og, cols, rows, mlstm_norm_w, B, S)
    x1, h2 = _outproj(yf, ym, x2, mod, norm2_w, w_out.astype(BF16), S)
    d_ff = w_down.shape[0]
    out = _ffn(h2, x1, mod, norm_f_w, _cast_w_up(w_up, d_ff), _interleave_gate_value(w_conv, d_ff),
               _interleave_gate_value(b_conv, d_ff).reshape(1, -1), w_down.astype(BF16), S)
    return out.reshape(B, S, D)
```

```python
import functools

import numpy as np
import jax
import jax.numpy as jnp
from jax import lax
from jax.experimental import pallas as pl
from jax.experimental.pallas import tpu as pltpu

F32 = jnp.float32
BF16 = jnp.bfloat16

N_MOD = 6
N_FOURIER_GROUPS = 4
N_HEADS = 4
RADIX = 4
EPS = 1e-6
M_INIT = -1e30
MLSTM_CHUNK = 128
N_GATE_LANES = 128
HALO = 16
V7X_VMEM_LIMIT = 56 * 1024 * 1024


def _params(sem, vmem=V7X_VMEM_LIMIT, flags=None):
    return pltpu.CompilerParams(dimension_semantics=sem, vmem_limit_bytes=vmem, flags=flags)


def _resident(block_shape, index_map):
    return pl.BlockSpec(block_shape, index_map, pipeline_mode=pl.Buffered(1))


def _adaln_kernel(c_ref, w_ref, b_ref, o_ref):
    c = c_ref[...]
    s = c * jax.nn.sigmoid(c)
    s_hi = s.astype(BF16).astype(F32)
    row = lax.broadcasted_iota(jnp.int32, s.shape, 0)
    lhs = jnp.where(row < 8, s_hi, s - s_hi).astype(BF16)
    w = w_ref[...]
    w_hi = w.astype(BF16)
    w_lo = (w - w_hi.astype(F32)).astype(BF16)
    r = jnp.dot(lhs, w_hi, preferred_element_type=F32)
    r = r + jnp.dot(lhs, w_lo, preferred_element_type=F32)
    o_ref[...] = r[0:8] + r[8:16] + b_ref[...]


def _adaln(c, w_ada, b_ada):
    B, D = c.shape
    N = w_ada.shape[1]
    tn = min(1024, D)
    cp =jnp.zeros((16, D), F32).at[0:B].set(c).at[8:8 + B].set(c)
    out = pl.pallas_call(
        _adaln_kernel,
        grid=(N // tn,),
        in_specs=[pl.BlockSpec((16, D), lambda j: (0, 0)),
                  pl.BlockSpec((D, tn), lambda j: (0, j)),
                  pl.BlockSpec((1, tn), lambda j: (0, j))],
        out_specs=pl.BlockSpec((8, tn), lambda j: (0, j)),
        out_shape=jax.ShapeDtypeStruct((8, N), F32),
        compiler_params=_params(("arbitrary",)),
        name="adaln",
    )(cp, w_ada, b_ada.reshape(1, N))
    return out[:B].reshape(B, N_MOD, D)


def _cast_in_kernel(w_ref, main_ref, gate_ref, *, n_main_blocks, n_gates):
    j = pl.program_id(0)

    @pl.when(j < n_main_blocks)
    def _():
        main_ref[...] = w_ref[...].astype(BF16)

    @pl.when(j == n_main_blocks)
    def _():
        w = w_ref[0:N_GATE_LANES, :]
        row = lax.broadcasted_iota(jnp.int32, w.shape, 0)
        gate_ref[...] = jnp.where(row < n_gates, w, 0.0).astype(BF16)


def _cast_w_in(w_in_t, n_main, n_gates):
    D = w_in_t.shape[1]
    rb = min(512, n_main)
    nb = n_main // rb
    kern = functools.partial(_cast_in_kernel, n_main_blocks=nb, n_gates=n_gates)
    return pl.pallas_call(
        kern,
        grid=(nb + 1,),
        in_specs=[pl.BlockSpec((rb, D), lambda j: (j, 0))],
        out_specs=[pl.BlockSpec((rb, D), lambda j: (jnp.minimum(j, nb - 1), 0)),
                   pl.BlockSpec((N_GATE_LANES, D), lambda j: (0, 0))],
        out_shape=[jax.ShapeDtypeStruct((n_main, D), BF16),
                   jax.ShapeDtypeStruct((N_GATE_LANES, D), BF16)],
        compiler_params=_params(("arbitrary",)),
        name="cast_w_in",
    )(w_in_t)


def _rms_mod(x, nw, shift, scale):
    ms = jnp.mean(x * x, axis=-1, keepdims=True)
    return x * lax.rsqrt(ms + EPS) * (nw * (1.0 + scale)) + shift


def _inproj_kernel(x_ref, mod_ref, nw_ref, w_ref, wg_ref, bg_ref,
                   uf_ref, q_ref, k_ref, v_ref, og_ref, g_ref, uf_scr, *, dims, qscale):
    d_f, d_qk, d_m = dims
    tm = x_ref.shape[0]
    h = _rms_mod(x_ref[...], nw_ref[...], mod_ref[0:1, :], mod_ref[1:2, :]).astype(BF16)

    nt = (((1,), (1,)), ((), ()))

    def proj(c0, width):
        return lax.dot_general(h, w_ref[c0:c0 + width, :], nt, preferred_element_type=F32)

    cw = min(512, d_qk)
    lw = uf_scr.shape[-1]
    for c0 in range(0, d_f, cw):
        u = proj(c0, cw)
        for l in range(cw // lw):
            uf_scr[l] = u[:, l * lw:(l + 1) * lw]
        for l in range(cw // lw):
            for r in range(RADIX):
                col = r * d_f + c0 + l * lw
                uf_ref[:, col:col + lw] = (
                    uf_scr[l, pl.ds(r, tm // RADIX, stride=RADIX), :].astype(BF16))
    o = d_f
    for c0 in range(0, d_qk, cw):
        q_ref[:, c0:c0 + cw] = (proj(o + c0, cw) * qscale).astype(BF16)
    o += d_qk
    for c0 in range(0, d_qk, cw):
        k_ref[c0:c0 + cw, :] = lax.dot_general(w_ref[o + c0:o + c0 + cw, :], h, nt,
                                               preferred_element_type=F32).astype(BF16)
    o += d_qk
    for c0 in range(0, d_m, cw):
        v_ref[:, c0:c0 + cw] = proj(o + c0, cw).astype(BF16)
    o += d_m
    for c0 in range(0, d_m, cw):
        og_ref[:, c0:c0 + cw] = jax.nn.sigmoid(proj(o + c0, cw)).astype(BF16)
    g_ref[...] = lax.dot_general(wg_ref[...], h, nt, preferred_element_type=F32) + bg_ref[...]


def _inproj(x2, mod, norm1_w, w_main, w_gate, b_gate, S, dims):
    T, D = x2.shape
    d_f, d_qk, d_m = dims
    tm = min(512, S)
    tiles_per_seq = S // tm
    n_main = w_main.shape[0]
    tok = lambda width: pl.BlockSpec((tm, width), lambda i: (i, 0))
    tok_t = lambda rows: pl.BlockSpec((None, rows, tm),
                                      lambda i: (i // tiles_per_seq, 0, i % tiles_per_seq))
    kern = functools.partial(_inproj_kernel, dims=dims,
                             qscale=float((d_qk // N_HEADS) ** -0.5))
    return pl.pallas_call(
        kern,
        grid=(T // tm,),
        in_specs=[tok(D),
                  pl.BlockSpec((None, N_MOD, D), lambda i: (i // tiles_per_seq, 0, 0)),
                  pl.BlockSpec((1, D), lambda i: (0, 0)),
                  _resident((n_main, D), lambda i: (0, 0)),
                  _resident((N_GATE_LANES, D), lambda i: (0, 0)),
                  pl.BlockSpec((N_GATE_LANES, 1), lambda i: (0, 0))],
        out_specs=[pl.BlockSpec((tm // RADIX, RADIX * d_f), lambda i: (i, 0)),
                   tok(d_qk), tok_t(d_qk), tok(d_m), tok(d_m), tok_t(N_GATE_LANES)],
        out_shape=[jax.ShapeDtypeStruct((T // RADIX, RADIX * d_f), BF16),
                   jax.ShapeDtypeStruct((T, d_qk), BF16),
                   jax.ShapeDtypeStruct((T // S, d_qk, S), BF16),
                   jax.ShapeDtypeStruct((T, d_m), BF16),
                   jax.ShapeDtypeStruct((T, d_m), BF16),
                   jax.ShapeDtypeStruct((T // S, N_GATE_LANES, S), F32)],
        scratch_shapes=[pltpu.VMEM((max(1, min(512, d_qk) // 128), tm, min(128, d_qk)), F32)],
        compiler_params=_params(("arbitrary",)),
        name="inproj",
    )(x2, mod, norm1_w.reshape(1, D), w_main, w_gate, b_gate)


def _dft_constants(S, gc, tk):
    n1 = S // RADIX
    idx = np.arange(n1, dtype=np.int64)
    ang = 2.0 * np.pi * ((idx[:, None] * idx[None, :]) % n1).astype(np.float64) / n1
    c, s = np.cos(ang), np.sin(ang)
    nt = n1 // tk
    cs = np.concatenate([c.reshape(nt, tk, n1), s.reshape(nt, tk, n1)], axis=1)
    tw = []
    for r in range(1, RADIX):
        a = 2.0 * np.pi * ((r * idx) % S).astype(np.float64) / S
        tw += [np.cos(a), np.sin(a)]
    tw = np.broadcast_to(np.stack(tw)[:, :, None], (2 * (RADIX - 1), n1, 128))
    ch = np.arange(gc, dtype=np.int64)
    angc = 2.0 * np.pi * ((ch[:, None] * ch[None, :]) % gc).astype(np.float64) / gc
    wch = np.concatenate([np.cos(angc), np.sin(angc)], axis=0) / np.sqrt(float(S) * gc)
    return (jnp.asarray(cs, dtype=BF16), jnp.asarray(np.ascontiguousarray(tw), dtype=F32),
            jnp.asarray(wch, dtype=BF16))


def _fourier_kernel(x_ref, cs_ref, tw_ref, wch_ref, o_ref, *, gc, n_groups, tk):
    n1 = x_ref.shape[0]
    d_f = gc * n_groups
    cs = cs_ref[...]
    wch = wch_ref[...]
    reps = gc // 128 if gc >= 128 else 1

    def lanes(t):
        if gc < 128:
            return t[:, :gc]
        return jnp.concatenate([t] * reps, axis=1) if reps > 1 else t

    c1, s1, c2, s2, c3, s3 = [lanes(tw_ref[i]) for i in range(2 * (RADIX - 1))]
    for g in range(n_groups):
        zc, zs = [], []
        for r in range(RADIX):
            z = jnp.dot(cs, x_ref[:, r * d_f + g * gc:r * d_f + (g + 1) * gc],
                        preferred_element_type=F32)
            zc.append(z[:tk])
            zs.append(z[tk:])
        t0r, t0i = zc[0], -zs[0]
        t1r, t1i = c1 * zc[1] - s1 * zs[1], -(c1 * zs[1] + s1 * zc[1])
        t2r, t2i = c2 * zc[2] - s2 * zs[2], -(c2 * zs[2] + s2 * zc[2])
        t3r, t3i = c3 * zc[3] - s3 * zs[3], -(c3 * zs[3] + s3 * zc[3])
        ar, ai = t0r + t2r, t0i + t2i
        br, bi = t0r - t2r, t0i - t2i
        cr, ci = t1r + t3r, t1i + t3i
        dr, di = t1r - t3r, t1i - t3i
        xq = [(ar + cr, ai + ci), (br + di, bi - dr), (ar - cr, ai - ci), (br - di, bi + dr)]
        for q in range(RADIX):
            xre, xim = xq[q]
            lhs = jnp.concatenate([xre.astype(BF16), xim.astype(BF16)], axis=1)
            y = jnp.dot(lhs, wch, preferred_element_type=F32)
            o_ref[q, :, g * gc:(g + 1) * gc] = y.astype(BF16)


def _fourier(uf4, B, S):
    d_f = uf4.shape[-1] // RADIX
    gc = d_f // N_FOURIER_GROUPS
    n1 = S // RADIX
    tk = min(512, n1)
    cs, tw, wch = _dft_constants(S, gc, tk)
    x4 = uf4.reshape(B, n1, RADIX * d_f)
    kern = functools.partial(_fourier_kernel, gc=gc, n_groups=N_FOURIER_GROUPS, tk=tk)
    out = pl.pallas_call(
        kern,
        grid=(B, n1 // tk),
        in_specs=[pl.BlockSpec((None, n1, RADIX * d_f), lambda b, t: (b, 0, 0)),
                  pl.BlockSpec((None, 2 * tk, n1), lambda b, t: (t, 0, 0)),
                  pl.BlockSpec((2 * (RADIX - 1), tk, 128), lambda b, t: (0, t, 0)),
                  pl.BlockSpec((2 * gc, gc), lambda b, t: (0, 0))],
        out_specs=pl.BlockSpec((None, RADIX, tk, d_f), lambda b, t: (b, 0, t, 0)),
        out_shape=jax.ShapeDtypeStruct((B, RADIX, n1, d_f), BF16),
        compiler_params=_params(("arbitrary", "arbitrary")),
        name="fourier",
    )(x4, cs, tw, wch)
    return out.reshape(B * S, d_f)


_ROW_R, _ROW_WK, _ROW_SOLD = 0, 2, 4
_ROW_COLS = 8
_COL_A, _COL_SINT, _COL_EML = 0, 1, 2
N_ROWS = 16


def _lane_scan(x, op, identity, reverse):
    n = x.shape[-1]
    lane = lax.broadcasted_iota(jnp.int32, x.shape, x.ndim - 1)
    shift = 1
    while shift < n:
        if reverse:
            rolled = pltpu.roll(x, n - shift, axis=x.ndim - 1)
            x = op(x, jnp.where(lane < n - shift, rolled, identity))
        else:
            rolled = pltpu.roll(x, shift, axis=x.ndim - 1)
            x = op(x, jnp.where(lane >= shift, rolled, identity))
        shift *= 2
    return x


def _gateprep_kernel(g_ref, o_ref, mst_ref, mnx_ref):
    B, H = o_ref.shape[0], o_ref.shape[1]
    nc, L = g_ref.shape[2], g_ref.shape[3]
    for d in range(2):
        reverse = d == 1
        rows = lambda first: jnp.concatenate(
            [g_ref[b, (2 * d + first) * H:(2 * d + first + 1) * H] for b in range(B)], axis=0)
        ig = rows(0)
        lf = jax.nn.log_sigmoid(rows(1))
        b_cum = _lane_scan(lf, jnp.add, 0.0, reverse)
        g = b_cum[:, :, 0:1] if reverse else b_cum[:, :, L - 1:L]
        r = ig - b_cum
        cm = _lane_scan(r, jnp.maximum, -jnp.inf, reverse)
        m_loc = g + jnp.max(r, axis=-1, keepdims=True)
        m = jnp.full((B * H, 1, 1), M_INIT, F32)
        for step in range(nc):
            c = nc - 1 - step if reverse else step
            mst_ref[:, c:c + 1, :] = m
            m = jnp.maximum(g[:, c:c + 1, :] + m, m_loc[:, c:c + 1, :])
            mnx_ref[:, c:c + 1, :] = m
        m_start, m_next = mst_ref[...], mnx_ref[...]
        top = jnp.maximum(cm, m_start)
        vals = {_ROW_R + d: r,
                _ROW_WK + d: jnp.exp(g + r - m_next),
                _ROW_SOLD + d: jnp.broadcast_to(jnp.exp(g + m_start - m_next), r.shape),
                _ROW_COLS + 4 * d + _COL_A: -top,
                _ROW_COLS + 4 * d + _COL_SINT: jnp.exp(m_start - top),
                _ROW_COLS + 4 * d + _COL_EML: jnp.exp(-(b_cum + top))}
        for row, val in vals.items():
            for b in range(B):
                for h in range(H):
                    o_ref[b, h, row] = val[b * H + h]
    zero = jnp.zeros((nc, L), F32)
    for row in (6, 7, _ROW_COLS + 3, _ROW_COLS + 7):
        for b in range(B):
            for h in range(H):
                o_ref[b, h, row] = zero


def _gateprep(gates_t, B, S):
    H, L = N_HEADS, min(MLSTM_CHUNK, S)
    nc = S // L
    rows = pl.pallas_call(
        _gateprep_kernel,
        out_shape=jax.ShapeDtypeStruct((B, H, N_ROWS, nc, L), F32),
        scratch_shapes=[pltpu.VMEM((B * H, nc, 1), F32), pltpu.VMEM((B * H, nc, 1), F32)],
        name="gateprep",
    )(gates_t[:, :4 * H].reshape(B, 4 * H, nc, L))
    rows = rows.reshape(B, H, N_ROWS, S)
    cols = jnp.swapaxes(rows[:, :, _ROW_COLS:, :], 2, 3)
    return rows, cols


DEN_LANES = 128


def _lane_tile(x, width):
    n = x.shape[1]
    if width % n == 0:
        return jnp.concatenate([x] * (width // n), axis=1) if width > n else x
    return jnp.broadcast_to(x[:, 0:1], (x.shape[0], width))


def _mlstm_kernel(q_ref, kt_ref, v_ref, og_ref, cols_ref, rows_ref, nw_ref, o_ref,
                  h_ref, c_ref, *, L):
    S, dk = q_ref.shape
    dv = v_ref.shape[1]
    nc = S // L
    jj = lax.broadcasted_iota(jnp.int32, (L, L), 0)
    tt = lax.broadcasted_iota(jnp.int32, (L, L), 1)
    ones = jnp.ones((L, DEN_LANES), BF16)
    c_ref[...] = jnp.zeros_like(c_ref)

    def step(i, _):
        dirs = (0, 1)
        sls = [pl.ds(pl.multiple_of(c * L, L), L) for c in (i, nc - 1 - i)]
        row = lambda d, k: rows_ref[k + d:k + d + 1, sls[d]]
        col = lambda d, k: cols_ref[sls[d], 4 * d + k:4 * d + k + 1]
        qc = [q_ref[sls[d], :] for d in dirs]
        ktc = [kt_ref[:, sls[d]] for d in dirs]
        vaug = [jnp.concatenate([v_ref[sls[d], :], ones], axis=1) for d in dirs]
        state = [c_ref[d] for d in dirs]
        s = [jnp.dot(qc[d], ktc[d], preferred_element_type=F32) for d in dirs]
        kwt = [(ktc[d].astype(F32) * row(d, _ROW_WK)).astype(BF16) for d in dirs]
        upd = [jnp.dot(kwt[d], vaug[d], preferred_element_type=F32) for d in dirs]
        for d in dirs:
            decay = _lane_tile(jnp.broadcast_to(row(d, _ROW_SOLD), (dk, L)), dv + DEN_LANES)
            c_ref[d] = decay * state[d] + upd[d]
        tot = []
        for d in dirs:
            allowed = (tt <= jj) if d == 0 else (tt >= jj)
            p = jnp.where(allowed, jnp.exp(col(d, _COL_A) + row(d, _ROW_R)), 0.0) * s[d]
            lhs = jnp.concatenate(
                [p.astype(BF16), (qc[d].astype(F32) * col(d, _COL_SINT)).astype(BF16)], axis=1)
            rhs = jnp.concatenate([vaug[d], state[d].astype(BF16)], axis=0)
            tot.append(jnp.dot(lhs, rhs, preferred_element_type=F32))
        for d in dirs:
            inv = 1.0 / jnp.maximum(jnp.abs(tot[d][:, dv:]), col(d, _COL_EML))
            h_ref[d, sls[d], :] = tot[d][:, :dv] * _lane_tile(inv, dv)
        return 0

    lax.fori_loop(0, nc, step, 0, unroll=4)

    fb = min(S, 4 * L)

    def finish(i, _):
        sl = pl.ds(pl.multiple_of(i * fb, fb), fb)
        h = h_ref[0, sl, :] + h_ref[1, sl, :]
        y = h * lax.rsqrt(jnp.mean(h * h, axis=-1, keepdims=True) + EPS)
        o_ref[sl, :] = (y * nw_ref[...] * og_ref[sl, :].astype(F32)).astype(BF16)
        return 0

    lax.fori_loop(0, S // fb, finish, 0)


def _mlstm(q, kt, v, og, rows, cols, norm_w, B, S):
    H = N_HEADS
    dk = q.shape[-1] // H
    dv = v.shape[-1] // H
    L = min(MLSTM_CHUNK, S)
    seq = lambda width: pl.BlockSpec((None, S, width), lambda b, h: (b, 0, h))
    kern = functools.partial(_mlstm_kernel, L=L)
    out = pl.pallas_call(
        kern,
        grid=(B, H),
        in_specs=[seq(dk),
                  pl.BlockSpec((None, dk, S), lambda b, h: (b, h, 0)),
                  seq(dv), seq(dv),
                  pl.BlockSpec((None, None, S, cols.shape[-1]), lambda b, h: (b, h, 0, 0)),
                  pl.BlockSpec((None, None, N_ROWS, S), lambda b, h: (b, h, 0, 0)),
                  pl.BlockSpec((1, dv), lambda b, h: (0, h))],
        out_specs=seq(dv),
        out_shape=jax.ShapeDtypeStruct((B, S, H * dv), BF16),
        scratch_shapes=[pltpu.VMEM((2, S, dv), F32),
                        pltpu.VMEM((2, dk, dv + DEN_LANES), F32)],
        compiler_params=_params(("arbitrary", "arbitrary")),
        name="mlstm",
    )(q.reshape(B, S, H * dk), kt, v.reshape(B, S, H * dv), og.reshape(B, S, H * dv),
      cols, rows, norm_w.reshape(1, H * dv))
    return out.reshape(B * S, H * dv)


def _outproj_kernel(yf_ref, ym_ref, x_ref, mod_ref, nw_ref, w_ref, x1_ref, h2_ref):
    d_f = yf_ref.shape[1]
    tm = x_ref.shape[0]
    rc = min(256, tm)
    for r0 in range(0, tm, rc):
        rs = slice(r0, r0 + rc)
        proj = jnp.dot(yf_ref[rs, :], w_ref[0:d_f, :], preferred_element_type=F32)
        proj = proj + jnp.dot(ym_ref[rs, :], w_ref[d_f:, :], preferred_element_type=F32)
        x1 = x_ref[rs, :] + mod_ref[2:3, :] * proj
        x1_ref[rs, :] = x1
        h2_ref[rs, :] = _rms_mod(x1, nw_ref[...], mod_ref[3:4, :], mod_ref[4:5, :]).astype(BF16)


def _outproj(yf, ym, x2, mod, norm2_w, w_out, S):
    T, D = x2.shape
    d_f, d_m = yf.shape[1], ym.shape[1]
    tm = min(512, S)
    tiles_per_seq = S // tm
    tok = lambda width: pl.BlockSpec((tm, width), lambda i: (i, 0))
    return pl.pallas_call(
        _outproj_kernel,
        grid=(T // tm,),
        in_specs=[tok(d_f), tok(d_m), tok(D),
                  pl.BlockSpec((None, N_MOD, D), lambda i: (i // tiles_per_seq, 0, 0)),
                  pl.BlockSpec((1, D), lambda i: (0, 0)),
                  _resident((d_f + d_m, D), lambda i: (0, 0))],
        out_specs=[tok(D), tok(D)],
        out_shape=[jax.ShapeDtypeStruct((T, D), F32), jax.ShapeDtypeStruct((T, D), BF16)],
        compiler_params=_params(("arbitrary",)),
        name="outproj",
    )(yf, ym, x2, mod, norm2_w.reshape(1, D), w_out)


FFN_COLS = 256
FFN_SUB = 2
LANES = 128
CONV_PHASES = 4


def _interleave_gate_value(w, d_ff):
    lead = w.shape[:-1]
    nb = d_ff // FFN_COLS
    w = w.reshape(lead + (2, nb, FFN_COLS))
    return jnp.swapaxes(w, -3, -2).reshape(lead + (2 * d_ff,))


def _cast_up_kernel(a_ref, v_ref, o_ref):
    for s in range(FFN_SUB):
        lo = s * FFN_COLS
        o_ref[:, 2 * lo:2 * lo + FFN_COLS] = a_ref[:, lo:lo + FFN_COLS].astype(BF16)
        o_ref[:, 2 * lo + FFN_COLS:2 * lo + 2 * FFN_COLS] = v_ref[:, lo:lo + FFN_COLS].astype(BF16)


def _cast_w_up(w_up, d_ff):
    D = w_up.shape[0]
    tn = FFN_SUB * FFN_COLS
    nj = d_ff // tn
    return pl.pallas_call(
        _cast_up_kernel,
        grid=(nj,),
        in_specs=[pl.BlockSpec((D, tn), lambda j: (0, j)),
                  pl.BlockSpec((D, tn), lambda j: (0, j + nj))],
        out_specs=pl.BlockSpec((D, 2 * tn), lambda j: (0, j)),
        out_shape=jax.ShapeDtypeStruct((D, 2 * d_ff), BF16),
        compiler_params=_params(("arbitrary",)),
        name="cast_w_up",
    )(w_up, w_up)


def _ffn_kernel(hp_ref, hc_ref, hn_ref, x1_ref, mod_ref, nfw_ref, wu_ref, cw_ref, cb_ref, wd_ref,
                o_ref, acc_ref, u_ref, y_ref, hext_ref, *, tiles_per_seq):
    i, j = pl.program_id(0), pl.program_id(1)
    tm = hc_ref.shape[0]

    @pl.when(j == 0)
    def _():
        acc_ref[...] = jnp.zeros_like(acc_ref)
        s_idx = i % tiles_per_seq
        hext_ref[0:HALO, :] = jnp.where(s_idx > 0, hp_ref[...], jnp.zeros_like(hp_ref))
        hext_ref[HALO:HALO + tm, :] = hc_ref[...]
        hext_ref[HALO + tm:, :] = jnp.where(s_idx < tiles_per_seq - 1, hn_ref[...],
                                            jnp.zeros_like(hn_ref))

    hext = hext_ref[...]

    w2 = 2 * FFN_COLS
    lw = u_ref.shape[-1]
    n_slab = w2 // lw
    rq = tm // CONV_PHASES
    for s in range(FFN_SUB):
        u = jnp.dot(hext, wu_ref[:, s * w2:(s + 1) * w2], preferred_element_type=F32)
        for l in range(n_slab):
            u_ref[s, l] = u[:, l * lw:(l + 1) * lw]
    for s in range(FFN_SUB):
        ys = []
        for l in range(n_slab):
            c0 = s * w2 + l * lw
            cw, cb = cw_ref[:, c0:c0 + lw], cb_ref[:, c0:c0 + lw]
            taps = [u_ref[s, l, pl.ds(HALO - 1 + m, rq, stride=CONV_PHASES), :]
                    for m in range(CONV_PHASES + 2)]
            ys.append(jnp.concatenate(
                [cw[0:1] * taps[k] + cw[1:2] * taps[k + 1] + cw[2:3] * taps[k + 2] + cb
                 for k in range(CONV_PHASES)], axis=0))
        y = jnp.concatenate(ys, axis=1)
        a, v = y[:, :FFN_COLS], y[:, FFN_COLS:]
        gact = (a * jax.nn.sigmoid(a) * v).astype(BF16)
        acc_ref[...] += jnp.dot(gact, wd_ref[s * FFN_COLS:(s + 1) * FFN_COLS, :],
                                preferred_element_type=F32)

    @pl.when(j == pl.num_programs(1) - 1)
    def _():
        D = o_ref.shape[1]
        sq = jnp.zeros((tm, lw), F32)
        for l in range(D // lw):
            cs = slice(l * lw, (l + 1) * lw)
            for k in range(CONV_PHASES):
                y_ref[l, pl.ds(k, rq, stride=CONV_PHASES), :] = acc_ref[k * rq:(k + 1) * rq, cs]
            y = x1_ref[:, cs] + mod_ref[5:6, cs] * y_ref[l]
            y_ref[l] = y
            sq = sq + y * y
        inv = lax.rsqrt(jnp.sum(sq, axis=-1, keepdims=True) / D + EPS)
        for l in range(D // lw):
            cs = slice(l * lw, (l + 1) * lw)
            o_ref[:, cs] = y_ref[l] * inv * nfw_ref[:, cs]


def _ffn(h2, x1, mod, norm_f_w, w_up, w_conv, b_conv, w_down, S):
    T, D = x1.shape
    d_ff = w_down.shape[0]
    tm = min(512, S)
    tn = FFN_SUB * FFN_COLS
    tiles_per_seq = S // tm
    hb = tm // HALO
    last = T // HALO - 1
    kern = functools.partial(_ffn_kernel, tiles_per_seq=tiles_per_seq)
    return pl.pallas_call(
        kern,
        grid=(T // tm, d_ff // tn),
        in_specs=[pl.BlockSpec((HALO, D), lambda i, j: (jnp.maximum(i * hb - 1, 0), 0)),
                  pl.BlockSpec((tm, D), lambda i, j: (i, 0)),
                  pl.BlockSpec((HALO, D), lambda i, j: (jnp.minimum((i + 1) * hb, last), 0)),
                  pl.BlockSpec((tm, D), lambda i, j: (i, 0)),
                  pl.BlockSpec((None, N_MOD, D), lambda i, j: (i // tiles_per_seq, 0, 0)),
                  pl.BlockSpec((1, D), lambda i, j: (0, 0)),
                  pl.BlockSpec((D, 2 * tn), lambda i, j: (0, j)),
                  pl.BlockSpec((3, 2 * tn), lambda i, j: (0, j)),
                  pl.BlockSpec((1, 2 * tn), lambda i, j: (0, j)),
                  pl.BlockSpec((tn, D), lambda i, j: (j, 0))],
        out_specs=pl.BlockSpec((tm, D), lambda i, j: (i, 0)),
        out_shape=jax.ShapeDtypeStruct((T, D), F32),
        scratch_shapes=[pltpu.VMEM((tm, D), F32),
                        pltpu.VMEM((FFN_SUB, 2 * FFN_COLS // LANES, tm + 2 * HALO, LANES), F32),
                        pltpu.VMEM((D // LANES, tm, LANES), F32),
                        pltpu.VMEM((tm + 2 * HALO, D), BF16)],
        compiler_params=_params(("arbitrary", "arbitrary")),
        name="ffn",
    )(h2, h2, h2, x1, mod, norm_f_w.reshape(1, D), w_up, w_conv, b_conv, w_down)


def kernel(x, c, w_ada, b_ada, norm1_w, w_in, b_gates, mlstm_norm_w, w_out, norm2_w, w_up,
           w_conv, b_conv, w_down, norm_f_w):
    B, S, D = x.shape
    d_f = D // 2
    d_m = D - d_f
    d_qk = d_m // 2
    dims = (d_f, d_qk, d_m)
    n_main = d_f + 2 * d_qk + 2 * d_m
    n_gates = 4 * N_HEADS
    assert w_in.shape[1] == n_main + n_gates and S % RADIX == 0

    x2 = x.reshape(B * S, D)
    w_main, w_gate = _cast_w_in(w_in.T, n_main, n_gates)
    b_gate = jnp.pad(b_gates, (0, N_GATE_LANES - n_gates)).reshape(N_GATE_LANES, 1)

    mod = _adaln(c, w_ada, b_ada)
    uf, q, kt, v, og, gates_t = _inproj(x2, mod, norm1_w, w_main, w_gate, b_gate, S, dims)
    yf = _fourier(uf, B, S)
    rows, cols = _gateprep(gates_t, B, S)
    ym = _mlstm(q, kt, v, og, rows, cols, mlstm_norm_w, B, S)
    x1, h2 = _outproj(yf, ym, x2, mod, norm2_w, w_out.astype(BF16), S)
    d_ff = w_down.shape[0]
    out = _ffn(h2, x1, mod, norm_f_w, _cast_w_up(w_up, d_ff), _interleave_gate_value(w_conv, d_ff),
               _interleave_gate_value(b_conv, d_ff).reshape(1, -1), w_down.astype(BF16), S)
    return out.reshape(B, S, D)
```

```python
import functools

import numpy as np
import jax
import jax.numpy as jnp
from jax import lax
from jax.experimental import pallas as pl
from jax.experimental.pallas import tpu as pltpu

F32 = jnp.float32
BF16 = jnp.bfloat16

N_MOD = 6
N_FOURIER_GROUPS = 4
N_HEADS = 4
RADIX = 4
EPS = 1e-6
M_INIT = -1e30
MLSTM_CHUNK = 128
N_GATE_LANES = 128
HALO = 16
V7X_VMEM_LIMIT = 56 * 1024 * 1024


def _params(sem, vmem=V7X_VMEM_LIMIT, flags=None):
    return pltpu.CompilerParams(dimension_semantics=sem, vmem_limit_bytes=vmem, flags=flags)


def _resident(block_shape, index_map):
    return pl.BlockSpec(block_shape, index_map, pipeline_mode=pl.Buffered(1))


def _adaln_kernel(c_ref, w_ref, b_ref, o_ref):
    c = c_ref[...]
    s = c * jax.nn.sigmoid(c)
    s_hi = s.astype(BF16).astype(F32)
    row = lax.broadcasted_iota(jnp.int32, s.shape, 0)
    lhs = jnp.where(row < 8, s_hi, s - s_hi).astype(BF16)
    w = w_ref[...]
    w_hi = w.astype(BF16)
    w_lo = (w - w_hi.astype(F32)).astype(BF16)
    r = jnp.dot(lhs, w_hi, preferred_element_type=F32)
    r = r + jnp.dot(lhs, w_lo, preferred_element_type=F32)
    o_ref[...] = r[0:8] + r[8:16] + b_ref[...]


def _adaln(c, w_ada, b_ada):
    B, D = c.shape
    N = w_ada.shape[1]
    tn = min(1024, D)
    cp =jnp.zeros((16, D), F32).at[0:B].set(c).at[8:8 + B].set(c)
    out = pl.pallas_call(
        _adaln_kernel,
        grid=(N // tn,),
        in_specs=[pl.BlockSpec((16, D), lambda j: (0, 0)),
                  pl.BlockSpec((D, tn), lambda j: (0, j)),
                  pl.BlockSpec((1, tn), lambda j: (0, j))],
        out_specs=pl.BlockSpec((8, tn), lambda j: (0, j)),
        out_shape=jax.ShapeDtypeStruct((8, N), F32),
        compiler_params=_params(("arbitrary",)),
        name="adaln",
    )(cp, w_ada, b_ada.reshape(1, N))
    return out[:B].reshape(B, N_MOD, D)


def _cast_in_kernel(w_ref, main_ref, gate_ref, *, n_main_blocks, n_gates):
    j = pl.program_id(0)

    @pl.when(j < n_main_blocks)
    def _():
        main_ref[...] = w_ref[...].astype(BF16)

    @pl.when(j == n_main_blocks)
    def _():
        w = w_ref[0:N_GATE_LANES, :]
        row = lax.broadcasted_iota(jnp.int32, w.shape, 0)
        gate_ref[...] = jnp.where(row < n_gates, w, 0.0).astype(BF16)


def _cast_w_in(w_in_t, n_main, n_gates):
    D = w_in_t.shape[1]
    rb = min(512, n_main)
    nb = n_main // rb
    kern = functools.partial(_cast_in_kernel, n_main_blocks=nb, n_gates=n_gates)
    return pl.pallas_call(
        kern,
        grid=(nb + 1,),
        in_specs=[pl.BlockSpec((rb, D), lambda j: (j, 0))],
        out_specs=[pl.BlockSpec((rb, D), lambda j: (jnp.minimum(j, nb - 1), 0)),
                   pl.BlockSpec((N_GATE_LANES, D), lambda j: (0, 0))],
        out_shape=[jax.ShapeDtypeStruct((n_main, D), BF16),
                   jax.ShapeDtypeStruct((N_GATE_LANES, D), BF16)],
        compiler_params=_params(("arbitrary",)),
        name="cast_w_in",
    )(w_in_t)


def _rms_mod(x, nw, shift, scale):
    ms = jnp.mean(x * x, axis=-1, keepdims=True)
    return x * lax.rsqrt(ms + EPS) * (nw * (1.0 + scale)) + shift


def _inproj_kernel(x_ref, mod_ref, nw_ref, w_ref, wg_ref, bg_ref, wo32_ref, wd32_ref,
                   uf_ref, q_ref, k_ref, v_ref, og_ref, g_ref, wo16_ref, wd16_ref, uf_scr,
                   *, dims, qscale):
    d_f, d_qk, d_m = dims
    tm = x_ref.shape[0]
    wo16_ref[...] = wo32_ref[...].astype(BF16)
    wd16_ref[...] = wd32_ref[...].astype(BF16)
    h = _rms_mod(x_ref[...], nw_ref[...], mod_ref[0:1, :], mod_ref[1:2, :]).astype(BF16)

    nt = (((1,), (1,)), ((), ()))

    def proj(c0, width):
        return lax.dot_general(h, w_ref[c0:c0 + width, :], nt, preferred_element_type=F32)

    cw = min(512, d_qk)
    lw = uf_scr.shape[-1]
    for c0 in range(0, d_f, cw):
        u = proj(c0, cw)
        for l in range(cw // lw):
            uf_scr[l] = u[:, l * lw:(l + 1) * lw]
        for l in range(cw // lw):
            for r in range(RADIX):
                col = r * d_f + c0 + l * lw
                uf_ref[:, col:col + lw] = (
                    uf_scr[l, pl.ds(r, tm // RADIX, stride=RADIX), :].astype(BF16))
    o = d_f
    for c0 in range(0, d_qk, cw):
        q_ref[:, c0:c0 + cw] = (proj(o + c0, cw) * qscale).astype(BF16)
    o += d_qk
    for c0 in range(0, d_qk, cw):
        k_ref[c0:c0 + cw, :] = lax.dot_general(w_ref[o + c0:o + c0 + cw, :], h, nt,
                                               preferred_element_type=F32).astype(BF16)
    o += d_qk
    for c0 in range(0, d_m, cw):
        v_ref[:, c0:c0 + cw] = proj(o + c0, cw).astype(BF16)
    o += d_m
    for c0 in range(0, d_m, cw):
        og_ref[:, c0:c0 + cw] = jax.nn.sigmoid(proj(o + c0, cw)).astype(BF16)
    g_ref[...] = lax.dot_general(wg_ref[...], h, nt, preferred_element_type=F32) + bg_ref[...]


def _row_slabs(w, n_steps):
    rows = w.shape[0] // n_steps
    assert rows * n_steps == w.shape[0] and rows % 16 == 0
    return pl.BlockSpec((rows, w.shape[1]), lambda i: (i, 0))


def _inproj(x2, mod, norm1_w, w_main, w_gate, b_gate, w_out, w_down, S, dims):
    T, D = x2.shape
    d_f, d_qk, d_m = dims
    tm = min(512, S)
    tiles_per_seq = S // tm
    n_main = w_main.shape[0]
    n_steps = T // tm
    tok = lambda width: pl.BlockSpec((tm, width), lambda i: (i, 0))
    tok_t = lambda rows: pl.BlockSpec((None, rows, tm),
                                      lambda i: (i // tiles_per_seq, 0, i % tiles_per_seq))
    kern = functools.partial(_inproj_kernel, dims=dims,
                             qscale=float((d_qk // N_HEADS) ** -0.5))
    return pl.pallas_call(
        kern,
        grid=(T // tm,),
        in_specs=[tok(D),
                  pl.BlockSpec((None, N_MOD, D), lambda i: (i // tiles_per_seq, 0, 0)),
                  pl.BlockSpec((1, D), lambda i: (0, 0)),
                  _resident((n_main, D), lambda i: (0, 0)),
                  _resident((N_GATE_LANES, D), lambda i: (0, 0)),
                  pl.BlockSpec((N_GATE_LANES, 1), lambda i: (0, 0)),
                  _row_slabs(w_out, n_steps), _row_slabs(w_down, n_steps)],
        out_specs=[pl.BlockSpec((tm // RADIX, RADIX * d_f), lambda i: (i, 0)),
                   tok(d_qk), tok_t(d_qk), tok(d_m), tok(d_m), tok_t(N_GATE_LANES),
                   _row_slabs(w_out, n_steps), _row_slabs(w_down, n_steps)],
        out_shape=[jax.ShapeDtypeStruct((T // RADIX, RADIX * d_f), BF16),
                   jax.ShapeDtypeStruct((T, d_qk), BF16),
                   jax.ShapeDtypeStruct((T // S, d_qk, S), BF16),
                   jax.ShapeDtypeStruct((T, d_m), BF16),
                   jax.ShapeDtypeStruct((T, d_m), BF16),
                   jax.ShapeDtypeStruct((T // S, N_GATE_LANES, S), F32),
                   jax.ShapeDtypeStruct(w_out.shape, BF16),
                   jax.ShapeDtypeStruct(w_down.shape, BF16)],
        scratch_shapes=[pltpu.VMEM((max(1, min(512, d_qk) // 128), tm, min(128, d_qk)), F32)],
        compiler_params=_params(("arbitrary",)),
        name="inproj",
    )(x2, mod, norm1_w.reshape(1, D), w_main, w_gate, b_gate, w_out, w_down)


def _dft_constants(S, gc, tk):
    n1 = S // RADIX
    idx = np.arange(n1, dtype=np.int64)
    ang = 2.0 * np.pi * ((idx[:, None] * idx[None, :]) % n1).astype(np.float64) / n1
    c, s = np.cos(ang), np.sin(ang)
    nt = n1 // tk
    cs = np.concatenate([c.reshape(nt, tk, n1), s.reshape(nt, tk, n1)], axis=1)
    tw = []
    for r in range(1, RADIX):
        a = 2.0 * np.pi * ((r * idx) % S).astype(np.float64) / S
        tw += [np.cos(a), np.sin(a)]
    tw = np.broadcast_to(np.stack(tw)[:, :, None], (2 * (RADIX - 1), n1, 128))
    ch = np.arange(gc, dtype=np.int64)
    angc = 2.0 * np.pi * ((ch[:, None] * ch[None, :]) % gc).astype(np.float64) / gc
    wch = np.concatenate([np.cos(angc), np.sin(angc)], axis=0) / np.sqrt(float(S) * gc)
    return (jnp.asarray(cs, dtype=BF16), jnp.asarray(np.ascontiguousarray(tw), dtype=F32),
            jnp.asarray(wch, dtype=BF16))


def _fourier_kernel(x_ref, cs_ref, tw_ref, wch_ref, o_ref, *, gc, n_groups, tk):
    n1 = x_ref.shape[0]
    d_f = gc * n_groups
    cs = cs_ref[...]
    wch = wch_ref[...]
    reps = gc // 128 if gc >= 128 else 1

    def lanes(t):
        if gc < 128:
            return t[:, :gc]
        return jnp.concatenate([t] * reps, axis=1) if reps > 1 else t

    c1, s1, c2, s2, c3, s3 = [lanes(tw_ref[i]) for i in range(2 * (RADIX - 1))]
    for g in range(n_groups):
        zc, zs = [], []
        for r in range(RADIX):
            z = jnp.dot(cs, x_ref[:, r * d_f + g * gc:r * d_f + (g + 1) * gc],
                        preferred_element_type=F32)
            zc.append(z[:tk])
            zs.append(z[tk:])
        t0r, t0i = zc[0], -zs[0]
        t1r, t1i = c1 * zc[1] - s1 * zs[1], -(c1 * zs[1] + s1 * zc[1])
        t2r, t2i = c2 * zc[2] - s2 * zs[2], -(c2 * zs[2] + s2 * zc[2])
        t3r, t3i = c3 * zc[3] - s3 * zs[3], -(c3 * zs[3] + s3 * zc[3])
        ar, ai = t0r + t2r, t0i + t2i
        br, bi = t0r - t2r, t0i - t2i
        cr, ci = t1r + t3r, t1i + t3i
        dr, di = t1r - t3r, t1i - t3i
        xq = [(ar + cr, ai + ci), (br + di, bi - dr), (ar - cr, ai - ci), (br - di, bi + dr)]
        for q in range(RADIX):
            xre, xim = xq[q]
            lhs = jnp.concatenate([xre.astype(BF16), xim.astype(BF16)], axis=1)
            y = jnp.dot(lhs, wch, preferred_element_type=F32)
            o_ref[q, :, g * gc:(g + 1) * gc] = y.astype(BF16)


def _fourier(uf4, B, S):
    d_f = uf4.shape[-1] // RADIX
    gc = d_f // N_FOURIER_GROUPS
    n1 = S // RADIX
    tk = min(512, n1)
    cs, tw, wch = _dft_constants(S, gc, tk)
    x4 = uf4.reshape(B, n1, RADIX * d_f)
    kern = functools.partial(_fourier_kernel, gc=gc, n_groups=N_FOURIER_GROUPS, tk=tk)
    out = pl.pallas_call(
        kern,
        grid=(B, n1 // tk),
        in_specs=[pl.BlockSpec((None, n1, RADIX * d_f), lambda b, t: (b, 0, 0)),
                  pl.BlockSpec((None, 2 * tk, n1), lambda b, t: (t, 0, 0)),
                  pl.BlockSpec((2 * (RADIX - 1), tk, 128), lambda b, t: (0, t, 0)),
                  pl.BlockSpec((2 * gc, gc), lambda b, t: (0, 0))],
        out_specs=pl.BlockSpec((None, RADIX, tk, d_f), lambda b, t: (b, 0, t, 0)),
        out_shape=jax.ShapeDtypeStruct((B, RADIX, n1, d_f), BF16),
        compiler_params=_params(("arbitrary", "arbitrary")),
        name="fourier",
    )(x4, cs, tw, wch)
    return out.reshape(B * S, d_f)


_ROW_R, _ROW_WK, _ROW_SOLD = 0, 2, 4
_ROW_COLS = 8
_COL_A, _COL_SINT, _COL_EML = 0, 1, 2
N_ROWS = 16


def _lane_scan(x, op, identity, reverse):
    n = x.shape[-1]
    lane = lax.broadcasted_iota(jnp.int32, x.shape, x.ndim - 1)
    shift = 1
    while shift < n:
        if reverse:
            rolled = pltpu.roll(x, n - shift, axis=x.ndim - 1)
            x = op(x, jnp.where(lane < n - shift, rolled, identity))
        else:
            rolled = pltpu.roll(x, shift, axis=x.ndim - 1)
            x = op(x, jnp.where(lane >= shift, rolled, identity))
        shift *= 2
    return x


def _gateprep_kernel(g_ref, o_ref, mst_ref, mnx_ref):
    B, H = o_ref.shape[0], o_ref.shape[1]
    nc, L = g_ref.shape[2], g_ref.shape[3]
    for d in range(2):
        reverse = d == 1
        rows = lambda first: jnp.concatenate(
            [g_ref[b, (2 * d + first) * H:(2 * d + first + 1) * H] for b in range(B)], axis=0)
        ig = rows(0)
        lf = jax.nn.log_sigmoid(rows(1))
        b_cum = _lane_scan(lf, jnp.add, 0.0, reverse)
        g = b_cum[:, :, 0:1] if reverse else b_cum[:, :, L - 1:L]
        r = ig - b_cum
        cm = _lane_scan(r, jnp.maximum, -jnp.inf, reverse)
        m_loc = g + jnp.max(r, axis=-1, keepdims=True)
        m = jnp.full((B * H, 1, 1), M_INIT, F32)
        for step in range(nc):
            c = nc - 1 - step if reverse else step
            mst_ref[:, c:c + 1, :] = m
            m = jnp.maximum(g[:, c:c + 1, :] + m, m_loc[:, c:c + 1, :])
            mnx_ref[:, c:c + 1, :] = m
        m_start, m_next = mst_ref[...], mnx_ref[...]
        top = jnp.maximum(cm, m_start)
        vals = {_ROW_R + d: r,
                _ROW_WK + d: jnp.exp(g + r - m_next),
                _ROW_SOLD + d: jnp.broadcast_to(jnp.exp(g + m_start - m_next), r.shape),
                _ROW_COLS + 4 * d + _COL_A: -top,
                _ROW_COLS + 4 * d + _COL_SINT: jnp.exp(m_start - top),
                _ROW_COLS + 4 * d + _COL_EML: jnp.exp(-(b_cum + top))}
        for row, val in vals.items():
            for b in range(B):
                for h in range(H):
                    o_ref[b, h, row] = val[b * H + h]
    zero = jnp.zeros((nc, L), F32)
    for row in (6, 7, _ROW_COLS + 3, _ROW_COLS + 7):
        for b in range(B):
            for h in range(H):
                o_ref[b, h, row] = zero


def _gateprep(gates_t, B, S):
    H, L = N_HEADS, min(MLSTM_CHUNK, S)
    nc = S // L
    rows = pl.pallas_call(
        _gateprep_kernel,
        out_shape=jax.ShapeDtypeStruct((B, H, N_ROWS, nc, L), F32),
        scratch_shapes=[pltpu.VMEM((B * H, nc, 1), F32), pltpu.VMEM((B * H, nc, 1), F32)],
        name="gateprep",
    )(gates_t[:, :4 * H].reshape(B, 4 * H, nc, L))
    rows = rows.reshape(B, H, N_ROWS, S)
    cols = jnp.swapaxes(rows[:, :, _ROW_COLS:, :], 2, 3)
    return rows, cols


DEN_LANES = 128


def _lane_tile(x, width):
    n = x.shape[1]
    if width % n == 0:
        return jnp.concatenate([x] * (width // n), axis=1) if width > n else x
    return jnp.broadcast_to(x[:, 0:1], (x.shape[0], width))


def _mlstm_kernel(q_ref, kt_ref, v_ref, og_ref, cols_ref, rows_ref, nw_ref, o_ref,
                  h_ref, c_ref, *, L):
    S, dk = q_ref.shape
    dv = v_ref.shape[1]
    nc = S // L
    jj = lax.broadcasted_iota(jnp.int32, (L, L), 0)
    tt = lax.broadcasted_iota(jnp.int32, (L, L), 1)
    ones = jnp.ones((L, DEN_LANES), BF16)
    c_ref[...] = jnp.zeros_like(c_ref)

    def step(i, _):
        dirs = (0, 1)
        sls = [pl.ds(pl.multiple_of(c * L, L), L) for c in (i, nc - 1 - i)]
        row = lambda d, k: rows_ref[k + d:k + d + 1, sls[d]]
        col = lambda d, k: cols_ref[sls[d], 4 * d + k:4 * d + k + 1]
        qc = [q_ref[sls[d], :] for d in dirs]
        ktc = [kt_ref[:, sls[d]] for d in dirs]
        vaug = [jnp.concatenate([v_ref[sls[d], :], ones], axis=1) for d in dirs]
        state = [c_ref[d] for d in dirs]
        s = [jnp.dot(qc[d], ktc[d], preferred_element_type=F32) for d in dirs]
        kwt = [(ktc[d].astype(F32) * row(d, _ROW_WK)).astype(BF16) for d in dirs]
        upd = [jnp.dot(kwt[d], vaug[d], preferred_element_type=F32) for d in dirs]
        for d in dirs:
            decay = _lane_tile(jnp.broadcast_to(row(d, _ROW_SOLD), (dk, L)), dv + DEN_LANES)
            c_ref[d] = decay * state[d] + upd[d]
        tot = []
        for d in dirs:
            allowed = (tt <= jj) if d == 0 else (tt >= jj)
            p = jnp.where(allowed, jnp.exp(col(d, _COL_A) + row(d, _ROW_R)), 0.0) * s[d]
            lhs = jnp.concatenate(
                [p.astype(BF16), (qc[d].astype(F32) * col(d, _COL_SINT)).astype(BF16)], axis=1)
            rhs = jnp.concatenate([vaug[d], state[d].astype(BF16)], axis=0)
            tot.append(jnp.dot(lhs, rhs, preferred_element_type=F32))
        for d in dirs:
            inv = 1.0 / jnp.maximum(jnp.abs(tot[d][:, dv:]), col(d, _COL_EML))
            h_ref[d, sls[d], :] = tot[d][:, :dv] * _lane_tile(inv, dv)
        return 0

    lax.fori_loop(0, nc, step, 0, unroll=4)

    fb = min(S, 4 * L)

    def finish(i, _):
        sl = pl.ds(pl.multiple_of(i * fb, fb), fb)
        h = h_ref[0, sl, :] + h_ref[1, sl, :]
        y = h * lax.rsqrt(jnp.mean(h * h, axis=-1, keepdims=True) + EPS)
        o_ref[sl, :] = (y * nw_ref[...] * og_ref[sl, :].astype(F32)).astype(BF16)
        return 0

    lax.fori_loop(0, S // fb, finish, 0)


def _mlstm(q, kt, v, og, rows, cols, norm_w, B, S):
    H = N_HEADS
    dk = q.shape[-1] // H
    dv = v.shape[-1] // H
    L = min(MLSTM_CHUNK, S)
    seq = lambda width: pl.BlockSpec((None, S, width), lambda b, h: (b, 0, h))
    kern = functools.partial(_mlstm_kernel, L=L)
    out = pl.pallas_call(
        kern,
        grid=(B, H),
        in_specs=[seq(dk),
                  pl.BlockSpec((None, dk, S), lambda b, h: (b, h, 0)),
                  seq(dv), seq(dv),
                  pl.BlockSpec((None, None, S, cols.shape[-1]), lambda b, h: (b, h, 0, 0)),
                  pl.BlockSpec((None, None, N_ROWS, S), lambda b, h: (b, h, 0, 0)),
                  pl.BlockSpec((1, dv), lambda b, h: (0, h))],
        out_specs=seq(dv),
        out_shape=jax.ShapeDtypeStruct((B, S, H * dv), BF16),
        scratch_shapes=[pltpu.VMEM((2, S, dv), F32),
                        pltpu.VMEM((2, dk, dv + DEN_LANES), F32)],
        compiler_params=_params(("arbitrary", "arbitrary")),
        name="mlstm",
    )(q.reshape(B, S, H * dk), kt, v.reshape(B, S, H * dv), og.reshape(B, S, H * dv),
      cols, rows, norm_w.reshape(1, H * dv))
    return out.reshape(B * S, H * dv)


def _outproj_kernel(yf_ref, ym_ref, x_ref, mod_ref, nw_ref, w_ref, wu32_ref,
                    x1_ref, h2_ref, wu16_ref):
    d_f = yf_ref.shape[1]
    tm = x_ref.shape[0]
    d_ff = wu32_ref.shape[1] // 2
    for g in range(d_ff // FFN_COLS):
        for half in range(2):
            src = half * d_ff + g * FFN_COLS
            dst = (2 * g + half) * FFN_COLS
            wu16_ref[:, dst:dst + FFN_COLS] = wu32_ref[:, src:src + FFN_COLS].astype(BF16)
    rc = min(256, tm)
    for r0 in range(0, tm, rc):
        rs = slice(r0, r0 + rc)
        proj = jnp.dot(yf_ref[rs, :], w_ref[0:d_f, :], preferred_element_type=F32)
        proj = proj + jnp.dot(ym_ref[rs, :], w_ref[d_f:, :], preferred_element_type=F32)
        x1 = x_ref[rs, :] + mod_ref[2:3, :] * proj
        x1_ref[rs, :] = x1
        h2_ref[rs, :] = _rms_mod(x1, nw_ref[...], mod_ref[3:4, :], mod_ref[4:5, :]).astype(BF16)


def _outproj(yf, ym, x2, mod, norm2_w, w_out, w_up, S):
    T, D = x2.shape
    d_f, d_m = yf.shape[1], ym.shape[1]
    tm = min(512, S)
    tiles_per_seq = S // tm
    n_steps = T // tm
    tok = lambda width: pl.BlockSpec((tm, width), lambda i: (i, 0))
    return pl.pallas_call(
        _outproj_kernel,
        grid=(n_steps,),
        in_specs=[tok(d_f), tok(d_m), tok(D),
                  pl.BlockSpec((None, N_MOD, D), lambda i: (i // tiles_per_seq, 0, 0)),
                  pl.BlockSpec((1, D), lambda i: (0, 0)),
                  _resident((d_f + d_m, D), lambda i: (0, 0)),
                  _row_slabs(w_up, n_steps)],
        out_specs=[tok(D), tok(D), _row_slabs(w_up, n_steps)],
        out_shape=[jax.ShapeDtypeStruct((T, D), F32), jax.ShapeDtypeStruct((T, D), BF16),
                   jax.ShapeDtypeStruct(w_up.shape, BF16)],
        compiler_params=_params(("arbitrary",)),
        name="outproj",
    )(yf, ym, x2, mod, norm2_w.reshape(1, D), w_out, w_up)


FFN_COLS = 256
FFN_SUB = 2
LANES = 128
CONV_PHASES = 4


def _interleave_gate_value(w, d_ff):
    lead = w.shape[:-1]
    nb = d_ff // FFN_COLS
    w = w.reshape(lead + (2, nb, FFN_COLS))
    return jnp.swapaxes(w, -3, -2).reshape(lead + (2 * d_ff,))


def _ffn_kernel(hp_ref, hc_ref, hn_ref, x1_ref, mod_ref, nfw_ref, wu_ref, cw_ref, cb_ref, wd_ref,
                o_ref, acc_ref, u_ref, y_ref, hext_ref, *, tiles_per_seq):
    i, j = pl.program_id(0), pl.program_id(1)
    tm = hc_ref.shape[0]

    @pl.when(j == 0)
    def _():
        acc_ref[...] = jnp.zeros_like(acc_ref)
        s_idx = i % tiles_per_seq
        hext_ref[0:HALO, :] = jnp.where(s_idx > 0, hp_ref[...], jnp.zeros_like(hp_ref))
        hext_ref[HALO:HALO + tm, :] = hc_ref[...]
        hext_ref[HALO + tm:, :] = jnp.where(s_idx < tiles_per_seq - 1, hn_ref[...],
                                            jnp.zeros_like(hn_ref))

    hext = hext_ref[...]

    w2 = 2 * FFN_COLS
    lw = u_ref.shape[-1]
    n_slab = w2 // lw
    rq = tm // CONV_PHASES
    for s in range(FFN_SUB):
        u = jnp.dot(hext, wu_ref[:, s * w2:(s + 1) * w2], preferred_element_type=F32)
        for l in range(n_slab):
            u_ref[s, l] = u[:, l * lw:(l + 1) * lw]
    for s in range(FFN_SUB):
        ys = []
        for l in range(n_slab):
            c0 = s * w2 + l * lw
            cw, cb = cw_ref[:, c0:c0 + lw], cb_ref[:, c0:c0 + lw]
            taps = [u_ref[s, l, pl.ds(HALO - 1 + m, rq, stride=CONV_PHASES), :]
                    for m in range(CONV_PHASES + 2)]
            ys.append(jnp.concatenate(
                [cw[0:1] * taps[k] + cw[1:2] * taps[k + 1] + cw[2:3] * taps[k + 2] + cb
                 for k in range(CONV_PHASES)], axis=0))
        y = jnp.concatenate(ys, axis=1)
        a, v = y[:, :FFN_COLS], y[:, FFN_COLS:]
        gact = (a * jax.nn.sigmoid(a) * v).astype(BF16)
        acc_ref[...] += jnp.dot(gact, wd_ref[s * FFN_COLS:(s + 1) * FFN_COLS, :],
                                preferred_element_type=F32)

    @pl.when(j == pl.num_programs(1) - 1)
    def _():
        D = o_ref.shape[1]
        sq = jnp.zeros((tm, lw), F32)
        for l in range(D // lw):
            cs = slice(l * lw, (l + 1) * lw)
            for k in range(CONV_PHASES):
                y_ref[l, pl.ds(k, rq, stride=CONV_PHASES), :] = acc_ref[k * rq:(k + 1) * rq, cs]
            y = x1_ref[:, cs] + mod_ref[5:6, cs] * y_ref[l]
            y_ref[l] = y
            sq = sq + y * y
        inv = lax.rsqrt(jnp.sum(sq, axis=-1, keepdims=True) / D + EPS)
        for l in range(D // lw):
            cs = slice(l * lw, (l + 1) * lw)
            o_ref[:, cs] = y_ref[l] * inv * nfw_ref[:, cs]


def _ffn(h2, x1, mod, norm_f_w, w_up, w_conv, b_conv, w_down, S):
    T, D = x1.shape
    d_ff = w_down.shape[0]
    tm = min(512, S)
    tn = FFN_SUB * FFN_COLS
    tiles_per_seq = S // tm
    hb = tm // HALO
    last = T // HALO - 1
    kern = functools.partial(_ffn_kernel, tiles_per_seq=tiles_per_seq)
    return pl.pallas_call(
        kern,
        grid=(T // tm, d_ff // tn),
        in_specs=[pl.BlockSpec((HALO, D), lambda i, j: (jnp.maximum(i * hb - 1, 0), 0)),
                  pl.BlockSpec((tm, D), lambda i, j: (i, 0)),
                  pl.BlockSpec((HALO, D), lambda i, j: (jnp.minimum((i + 1) * hb, last), 0)),
                  pl.BlockSpec((tm, D), lambda i, j: (i, 0)),
                  pl.BlockSpec((None, N_MOD, D), lambda i, j: (i // tiles_per_seq, 0, 0)),
                  pl.BlockSpec((1, D), lambda i, j: (0, 0)),
                  pl.BlockSpec((D, 2 * tn), lambda i, j: (0, j)),
                  pl.BlockSpec((3, 2 * tn), lambda i, j: (0, j)),
                  pl.BlockSpec((1, 2 * tn), lambda i, j: (0, j)),
                  pl.BlockSpec((tn, D), lambda i, j: (j, 0))],
        out_specs=pl.BlockSpec((tm, D), lambda i, j: (i, 0)),
        out_shape=jax.ShapeDtypeStruct((T, D), F32),
        scratch_shapes=[pltpu.VMEM((tm, D), F32),
                        pltpu.VMEM((FFN_SUB, 2 * FFN_COLS // LANES, tm + 2 * HALO, LANES), F32),
                        pltpu.VMEM((D // LANES, tm, LANES), F32),
                        pltpu.VMEM((tm + 2 * HALO, D), BF16)],
        compiler_params=_params(("arbitrary", "arbitrary")),
        name="ffn",
    )(h2, h2, h2, x1, mod, norm_f_w.reshape(1, D), w_up, w_conv, b_conv, w_down)


def kernel(x, c, w_ada, b_ada, norm1_w, w_in, b_gates, mlstm_norm_w, w_out, norm2_w, w_up,
           w_conv, b_conv, w_down, norm_f_w):
    B, S, D = x.shape
    d_f = D // 2
    d_m = D - d_f
    d_qk = d_m // 2
    dims = (d_f, d_qk, d_m)
    n_main = d_f + 2 * d_qk + 2 * d_m
    n_gates = 4 * N_HEADS
    assert w_in.shape[1] == n_main + n_gates and S % RADIX == 0

    x2 = x.reshape(B * S, D)
    w_main, w_gate = _cast_w_in(w_in.T, n_main, n_gates)
    b_gate = jnp.pad(b_gates, (0, N_GATE_LANES - n_gates)).reshape(N_GATE_LANES, 1)

    mod = _adaln(c, w_ada, b_ada)
    uf, q, kt, v, og, gates_t, w_out16, w_down16 = _inproj(
        x2, mod, norm1_w, w_main, w_gate, b_gate, w_out, w_down, S, dims)
    yf = _fourier(uf, B, S)
    rows, cols = _gateprep(gates_t, B, S)
    ym = _mlstm(q, kt, v, og, rows, cols, mlstm_norm_w, B, S)
    x1, h2, w_up16 = _outproj(yf, ym, x2, mod, norm2_w, w_out16, w_up, S)
    d_ff = w_down.shape[0]
    out = _ffn(h2, x1, mod, norm_f_w, w_up16, _interleave_gate_value(w_conv, d_ff),
               _interleave_gate_value(b_conv, d_ff).reshape(1, -1), w_down16, S)
    return out.reshape(B, S, D)
```

```python
import functools

import numpy as np
import jax
import jax.numpy as jnp
from jax import lax
from jax.experimental import pallas as pl
from jax.experimental.pallas import tpu as pltpu

F32 = jnp.float32
BF16 = jnp.bfloat16

N_MOD = 6
N_FOURIER_GROUPS = 4
N_HEADS = 4
RADIX = 4
EPS = 1e-6
M_INIT = -1e30
MLSTM_CHUNK = 128
N_GATE_LANES = 128
HALO = 16
V7X_VMEM_LIMIT = 56 * 1024 * 1024


def _params(sem, vmem=V7X_VMEM_LIMIT, flags=None):
    return pltpu.CompilerParams(dimension_semantics=sem, vmem_limit_bytes=vmem, flags=flags)


def _resident(block_shape, index_map):
    return pl.BlockSpec(block_shape, index_map, pipeline_mode=pl.Buffered(1))


def _adaln_kernel(c_ref, w_ref, b_ref, o_ref):
    c = c_ref[...]
    s = c * jax.nn.sigmoid(c)
    s_hi = s.astype(BF16).astype(F32)
    row = lax.broadcasted_iota(jnp.int32, s.shape, 0)
    lhs = jnp.where(row < 8, s_hi, s - s_hi).astype(BF16)
    w = w_ref[...]
    w_hi = w.astype(BF16)
    w_lo = (w - w_hi.astype(F32)).astype(BF16)
    r = jnp.dot(lhs, w_hi, preferred_element_type=F32)
    r = r + jnp.dot(lhs, w_lo, preferred_element_type=F32)
    o_ref[...] = r[0:8] + r[8:16] + b_ref[...]


def _adaln(c, w_ada, b_ada):
    B, D = c.shape
    N = w_ada.shape[1]
    tn = min(1024, D)
    cp =jnp.zeros((16, D), F32).at[0:B].set(c).at[8:8 + B].set(c)
    out = pl.pallas_call(
        _adaln_kernel,
        grid=(N // tn,),
        in_specs=[pl.BlockSpec((16, D), lambda j: (0, 0)),
                  pl.BlockSpec((D, tn), lambda j: (0, j)),
                  pl.BlockSpec((1, tn), lambda j: (0, j))],
        out_specs=pl.BlockSpec((8, tn), lambda j: (0, j)),
        out_shape=jax.ShapeDtypeStruct((8, N), F32),
        compiler_params=_params(("arbitrary",)),
        name="adaln",
    )(cp, w_ada, b_ada.reshape(1, N))
    return out[:B].reshape(B, N_MOD, D)


def _cast_in_kernel(w_ref, main_ref, gate_ref, *, n_main_blocks, n_gates):
    j = pl.program_id(0)

    @pl.when(j < n_main_blocks)
    def _():
        main_ref[...] = w_ref[...].astype(BF16)

    @pl.when(j == n_main_blocks)
    def _():
        w = w_ref[0:N_GATE_LANES, :]
        row = lax.broadcasted_iota(jnp.int32, w.shape, 0)
        gate_ref[...] = jnp.where(row < n_gates, w, 0.0).astype(BF16)


def _cast_w_in(w_in_t, n_main, n_gates):
    D = w_in_t.shape[1]
    rb = min(512, n_main)
    nb = n_main // rb
    kern = functools.partial(_cast_in_kernel, n_main_blocks=nb, n_gates=n_gates)
    return pl.pallas_call(
        kern,
        grid=(nb + 1,),
        in_specs=[pl.BlockSpec((rb, D), lambda j: (j, 0))],
        out_specs=[pl.BlockSpec((rb, D), lambda j: (jnp.minimum(j, nb - 1), 0)),
                   pl.BlockSpec((N_GATE_LANES, D), lambda j: (0, 0))],
        out_shape=[jax.ShapeDtypeStruct((n_main, D), BF16),
                   jax.ShapeDtypeStruct((N_GATE_LANES, D), BF16)],
        compiler_params=_params(("arbitrary",)),
        name="cast_w_in",
    )(w_in_t)


def _rms_mod(x, nw, shift, scale):
    ms = jnp.mean(x * x, axis=-1, keepdims=True)
    return x * lax.rsqrt(ms + EPS) * (nw * (1.0 + scale)) + shift


def _cast_up_slab(src_ref, dst_ref):
    d_ff = src_ref.shape[1] // 2
    for g in range(d_ff // FFN_COLS):
        for half in range(2):
            src = half * d_ff + g * FFN_COLS
            dst = (2 * g + half) * FFN_COLS
            dst_ref[:, dst:dst + FFN_COLS] = src_ref[:, src:src + FFN_COLS].astype(BF16)


def _inproj_kernel(x_ref, mod_ref, nw_ref, w_ref, wg_ref, bg_ref, wo32_ref, wu32_ref,
                   uf_ref, q_ref, k_ref, v_ref, og_ref, g_ref, wo16_ref, wu16_ref, uf_scr,
                   *, dims, qscale):
    d_f, d_qk, d_m = dims
    tm = x_ref.shape[0]
    wo16_ref[...] = wo32_ref[...].astype(BF16)
    _cast_up_slab(wu32_ref, wu16_ref)
    h = _rms_mod(x_ref[...], nw_ref[...], mod_ref[0:1, :], mod_ref[1:2, :]).astype(BF16)

    nt = (((1,), (1,)), ((), ()))

    def proj(c0, width):
        return lax.dot_general(h, w_ref[c0:c0 + width, :], nt, preferred_element_type=F32)

    cw = min(512, d_qk)
    lw = uf_scr.shape[-1]
    for c0 in range(0, d_f, cw):
        u = proj(c0, cw)
        for l in range(cw // lw):
            uf_scr[l] = u[:, l * lw:(l + 1) * lw]
        for l in range(cw // lw):
            for r in range(RADIX):
                col = r * d_f + c0 + l * lw
                uf_ref[:, col:col + lw] = (
                    uf_scr[l, pl.ds(r, tm // RADIX, stride=RADIX), :].astype(BF16))
    def put_heads(ref, c0, val):
        dh = ref.shape[-1]
        for c in range(0, val.shape[1], dh):
            ref[(c0 + c) // dh] = val[:, c:c + dh]

    o = d_f
    for c0 in range(0, d_qk, cw):
        put_heads(q_ref, c0, (proj(o + c0, cw) * qscale).astype(BF16))
    o += d_qk
    for c0 in range(0, d_qk, cw):
        k_ref[c0:c0 + cw, :] = lax.dot_general(w_ref[o + c0:o + c0 + cw, :], h, nt,
                                               preferred_element_type=F32).astype(BF16)
    o += d_qk
    for c0 in range(0, d_m, cw):
        put_heads(v_ref, c0, proj(o + c0, cw).astype(BF16))
    o += d_m
    for c0 in range(0, d_m, cw):
        put_heads(og_ref, c0, jax.nn.sigmoid(proj(o + c0, cw)).astype(BF16))
    g_ref[...] = lax.dot_general(wg_ref[...], h, nt, preferred_element_type=F32) + bg_ref[...]


def _row_slabs(w, n_steps, first_row=0, n_rows=None, step=lambda *ids: ids[0]):
    n_rows = w.shape[0] - first_row if n_rows is None else n_rows
    rows = n_rows // n_steps
    assert rows * n_steps == n_rows and rows % 16 == 0 and first_row % rows == 0
    return pl.BlockSpec((rows, w.shape[1]), lambda *ids: (step(*ids) + first_row // rows, 0))


def _inproj(x2, mod, norm1_w, w_main, w_gate, b_gate, w_out, w_up, S, dims):
    T, D = x2.shape
    d_f, d_qk, d_m = dims
    tm = min(512, S)
    tiles_per_seq = S // tm
    n_main = w_main.shape[0]
    n_steps = T // tm
    tok = lambda width: pl.BlockSpec((tm, width), lambda i: (i, 0))
    tok_t = lambda rows: pl.BlockSpec((None, rows, tm),
                                      lambda i: (i // tiles_per_seq, 0, i % tiles_per_seq))
    H = N_HEADS
    tok_h = lambda dh: pl.BlockSpec((None, H, tm, dh),
                                    lambda i: (i // tiles_per_seq, 0, i % tiles_per_seq, 0))
    kern = functools.partial(_inproj_kernel, dims=dims,
                             qscale=float((d_qk // N_HEADS) ** -0.5))
    return pl.pallas_call(
        kern,
        grid=(T // tm,),
        in_specs=[tok(D),
                  pl.BlockSpec((None, N_MOD, D), lambda i: (i // tiles_per_seq, 0, 0)),
                  pl.BlockSpec((1, D), lambda i: (0, 0)),
                  _resident((n_main, D), lambda i: (0, 0)),
                  _resident((N_GATE_LANES, D), lambda i: (0, 0)),
                  pl.BlockSpec((N_GATE_LANES, 1), lambda i: (0, 0)),
                  _row_slabs(w_out, n_steps), _row_slabs(w_up, n_steps, 0, w_up.shape[0] // 2)],
        out_specs=[pl.BlockSpec((tm // RADIX, RADIX * d_f), lambda i: (i, 0)),
                   tok_h(d_qk // H), tok_t(d_qk), tok_h(d_m // H), tok_h(d_m // H),
                   tok_t(N_GATE_LANES),
                   _row_slabs(w_out, n_steps), _row_slabs(w_up, n_steps, 0, w_up.shape[0] // 2)],
        out_shape=[jax.ShapeDtypeStruct((T // RADIX, RADIX * d_f), BF16),
                   jax.ShapeDtypeStruct((T // S, H, S, d_qk // H), BF16),
                   jax.ShapeDtypeStruct((T // S, d_qk, S), BF16),
                   jax.ShapeDtypeStruct((T // S, H, S, d_m // H), BF16),
                   jax.ShapeDtypeStruct((T // S, H, S, d_m // H), BF16),
                   jax.ShapeDtypeStruct((T // S, N_GATE_LANES, S), F32),
                   jax.ShapeDtypeStruct(w_out.shape, BF16),
                   jax.ShapeDtypeStruct((w_up.shape[0] // 2, w_up.shape[1]), BF16)],
        scratch_shapes=[pltpu.VMEM((max(1, min(512, d_qk) // 128), tm, min(128, d_qk)), F32)],
        compiler_params=_params(("arbitrary",)),
        name="inproj",
    )(x2, mod, norm1_w.reshape(1, D), w_main, w_gate, b_gate, w_out, w_up)


def _dft_constants(S, gc, tk):
    n1 = S // RADIX
    idx = np.arange(n1, dtype=np.int64)
    ang = 2.0 * np.pi * ((idx[:, None] * idx[None, :]) % n1).astype(np.float64) / n1
    c, s = np.cos(ang), np.sin(ang)
    nt = n1 // tk
    cs = np.concatenate([c.reshape(nt, tk, n1), s.reshape(nt, tk, n1)], axis=1)
    tw = []
    for r in range(1, RADIX):
        a = 2.0 * np.pi * ((r * idx) % S).astype(np.float64) / S
        tw += [np.cos(a), np.sin(a)]
    tw = np.broadcast_to(np.stack(tw)[:, :, None], (2 * (RADIX - 1), n1, 128))
    ch = np.arange(gc, dtype=np.int64)
    angc = 2.0 * np.pi * ((ch[:, None] * ch[None, :]) % gc).astype(np.float64) / gc
    wch = np.concatenate([np.cos(angc), np.sin(angc)], axis=0) / np.sqrt(float(S) * gc)
    return (jnp.asarray(cs, dtype=BF16), jnp.asarray(np.ascontiguousarray(tw), dtype=F32),
            jnp.asarray(wch, dtype=BF16))


def _fourier_kernel(x_ref, cs_ref, tw_ref, wch_ref, o_ref, *, gc, n_groups, tk):
    n1 = x_ref.shape[0]
    d_f = gc * n_groups
    cs = cs_ref[...]
    wch = wch_ref[...]
    reps = gc // 128 if gc >= 128 else 1

    def lanes(t):
        if gc < 128:
            return t[:, :gc]
        return jnp.concatenate([t] * reps, axis=1) if reps > 1 else t

    c1, s1, c2, s2, c3, s3 = [lanes(tw_ref[i]) for i in range(2 * (RADIX - 1))]
    for g in range(n_groups):
        zc, zs = [], []
        for r in range(RADIX):
            z = jnp.dot(cs, x_ref[:, r * d_f + g * gc:r * d_f + (g + 1) * gc],
                        preferred_element_type=F32)
            zc.append(z[:tk])
            zs.append(z[tk:])
        t0r, t0i = zc[0], -zs[0]
        t1r, t1i = c1 * zc[1] - s1 * zs[1], -(c1 * zs[1] + s1 * zc[1])
        t2r, t2i = c2 * zc[2] - s2 * zs[2], -(c2 * zs[2] + s2 * zc[2])
        t3r, t3i = c3 * zc[3] - s3 * zs[3], -(c3 * zs[3] + s3 * zc[3])
        ar, ai = t0r + t2r, t0i + t2i
        br, bi = t0r - t2r, t0i - t2i
        cr, ci = t1r + t3r, t1i + t3i
        dr, di = t1r - t3r, t1i - t3i
        xq = [(ar + cr, ai + ci), (br + di, bi - dr), (ar - cr, ai - ci), (br - di, bi + dr)]
        for q in range(RADIX):
            xre, xim = xq[q]
            lhs = jnp.concatenate([xre.astype(BF16), xim.astype(BF16)], axis=1)
            y = jnp.dot(lhs, wch, preferred_element_type=F32)
            o_ref[q, :, g * gc:(g + 1) * gc] = y.astype(BF16)


def _fourier(uf4, B, S):
    d_f = uf4.shape[-1] // RADIX
    gc = d_f // N_FOURIER_GROUPS
    n1 = S // RADIX
    tk = min(512, n1)
    cs, tw, wch = _dft_constants(S, gc, tk)
    x4 = uf4.reshape(B, n1, RADIX * d_f)
    kern = functools.partial(_fourier_kernel, gc=gc, n_groups=N_FOURIER_GROUPS, tk=tk)
    out = pl.pallas_call(
        kern,
        grid=(B, n1 // tk),
        in_specs=[pl.BlockSpec((None, n1, RADIX * d_f), lambda b, t: (b, 0, 0)),
                  pl.BlockSpec((None, 2 * tk, n1), lambda b, t: (t, 0, 0)),
                  pl.BlockSpec((2 * (RADIX - 1), tk, 128), lambda b, t: (0, t, 0)),
                  pl.BlockSpec((2 * gc, gc), lambda b, t: (0, 0))],
        out_specs=pl.BlockSpec((None, RADIX, tk, d_f), lambda b, t: (b, 0, t, 0)),
        out_shape=jax.ShapeDtypeStruct((B, RADIX, n1, d_f), BF16),
        compiler_params=_params(("arbitrary", "arbitrary")),
        name="fourier",
    )(x4, cs, tw, wch)
    return out.reshape(B * S, d_f)


_ROW_R, _ROW_WK, _ROW_SOLD = 0, 2, 4
_ROW_COLS = 8
_COL_A, _COL_SINT, _COL_EML = 0, 1, 2
N_ROWS = 16


def _lane_scan(x, op, identity, reverse):
    n = x.shape[-1]
    lane = lax.broadcasted_iota(jnp.int32, x.shape, x.ndim - 1)
    shift = 1
    while shift < n:
        if reverse:
            rolled = pltpu.roll(x, n - shift, axis=x.ndim - 1)
            x = op(x, jnp.where(lane < n - shift, rolled, identity))
        else:
            rolled = pltpu.roll(x, shift, axis=x.ndim - 1)
            x = op(x, jnp.where(lane >= shift, rolled, identity))
        shift *= 2
    return x


def _gateprep_kernel(g_ref, o_ref, mst_ref, mnx_ref):
    B, H = o_ref.shape[0], o_ref.shape[1]
    nc, L = g_ref.shape[2], g_ref.shape[3]
    for d in range(2):
        reverse = d == 1
        rows = lambda first: jnp.concatenate(
            [g_ref[b, (2 * d + first) * H:(2 * d + first + 1) * H] for b in range(B)], axis=0)
        ig = rows(0)
        lf = jax.nn.log_sigmoid(rows(1))
        b_cum = _lane_scan(lf, jnp.add, 0.0, reverse)
        g = b_cum[:, :, 0:1] if reverse else b_cum[:, :, L - 1:L]
        r = ig - b_cum
        cm = _lane_scan(r, jnp.maximum, -jnp.inf, reverse)
        m_loc = g + jnp.max(r, axis=-1, keepdims=True)
        m = jnp.full((B * H, 1, 1), M_INIT, F32)
        for step in range(nc):
            c = nc - 1 - step if reverse else step
            mst_ref[:, c:c + 1, :] = m
            m = jnp.maximum(g[:, c:c + 1, :] + m, m_loc[:, c:c + 1, :])
            mnx_ref[:, c:c + 1, :] = m
        m_start, m_next = mst_ref[...], mnx_ref[...]
        top = jnp.maximum(cm, m_start)
        vals = {_ROW_R + d: r,
                _ROW_WK + d: jnp.exp(g + r - m_next),
                _ROW_SOLD + d: jnp.broadcast_to(jnp.exp(g + m_start - m_next), r.shape),
                _ROW_COLS + 4 * d + _COL_A: -top,
                _ROW_COLS + 4 * d + _COL_SINT: jnp.exp(m_start - top),
                _ROW_COLS + 4 * d + _COL_EML: jnp.exp(-(b_cum + top))}
        for row, val in vals.items():
            for b in range(B):
                for h in range(H):
                    o_ref[b, h, row] = val[b * H + h]
    zero = jnp.zeros((nc, L), F32)
    for row in (6, 7, _ROW_COLS + 3, _ROW_COLS + 7):
        for b in range(B):
            for h in range(H):
                o_ref[b, h, row] = zero


def _gateprep(gates_t, B, S):
    H, L = N_HEADS, min(MLSTM_CHUNK, S)
    nc = S // L
    rows = pl.pallas_call(
        _gateprep_kernel,
        out_shape=jax.ShapeDtypeStruct((B, H, N_ROWS, nc, L), F32),
        scratch_shapes=[pltpu.VMEM((B * H, nc, 1), F32), pltpu.VMEM((B * H, nc, 1), F32)],
        name="gateprep",
    )(gates_t[:, :4 * H].reshape(B, 4 * H, nc, L))
    rows = rows.reshape(B, H, N_ROWS, S)
    cols = jnp.swapaxes(rows[:, :, _ROW_COLS:, :], 2, 3)
    return rows, cols


DEN_LANES = 128


def _lane_tile(x, width):
    n = x.shape[1]
    if width % n == 0:
        return jnp.concatenate([x] * (width // n), axis=1) if width > n else x
    return jnp.broadcast_to(x[:, 0:1], (x.shape[0], width))


def _mlstm_kernel(q_ref, kt_ref, v_ref, og_ref, cols_ref, rows_ref, nw_ref, wd32_ref,
                  o_ref, wd16_ref, h_ref, c_ref, *, L):
    S, dk = q_ref.shape
    dv = v_ref.shape[1]
    nc = S // L
    jj = lax.broadcasted_iota(jnp.int32, (L, L), 0)
    tt = lax.broadcasted_iota(jnp.int32, (L, L), 1)
    ones = jnp.ones((L, DEN_LANES), BF16)
    c_ref[...] = jnp.zeros_like(c_ref)
    wd16_ref[...] = wd32_ref[...].astype(BF16)

    def step(i, _):
        dirs = (0, 1)
        sls = [pl.ds(pl.multiple_of(c * L, L), L) for c in (i, nc - 1 - i)]
        row = lambda d, k: rows_ref[k + d:k + d + 1, sls[d]]
        col = lambda d, k: cols_ref[sls[d], 4 * d + k:4 * d + k + 1]
        qc = [q_ref[sls[d], :] for d in dirs]
        ktc = [kt_ref[:, sls[d]] for d in dirs]
        vaug = [jnp.concatenate([v_ref[sls[d], :], ones], axis=1) for d in dirs]
        state = [c_ref[d] for d in dirs]
        s = [jnp.dot(qc[d], ktc[d], preferred_element_type=F32) for d in dirs]
        kwt = [(ktc[d].astype(F32) * row(d, _ROW_WK)).astype(BF16) for d in dirs]
        upd = [jnp.dot(kwt[d], vaug[d], preferred_element_type=F32) for d in dirs]
        for d in dirs:
            decay = _lane_tile(jnp.broadcast_to(row(d, _ROW_SOLD), (dk, L)), dv + DEN_LANES)
            c_ref[d] = decay * state[d] + upd[d]
        tot = []
        for d in dirs:
            allowed = (tt <= jj) if d == 0 else (tt >= jj)
            p = jnp.where(allowed, jnp.exp(col(d, _COL_A) + row(d, _ROW_R)), 0.0) * s[d]
            lhs = jnp.concatenate(
                [p.astype(BF16), (qc[d].astype(F32) * col(d, _COL_SINT)).astype(BF16)], axis=1)
            rhs = jnp.concatenate([vaug[d], state[d].astype(BF16)], axis=0)
            tot.append(jnp.dot(lhs, rhs, preferred_element_type=F32))
        for d in dirs:
            inv = 1.0 / jnp.maximum(jnp.abs(tot[d][:, dv:]), col(d, _COL_EML))
            h_ref[d, sls[d], :] = tot[d][:, :dv] * _lane_tile(inv, dv)
        return 0

    lax.fori_loop(0, nc, step, 0, unroll=4)

    fb = min(S, 4 * L)

    def finish(i, _):
        sl = pl.ds(pl.multiple_of(i * fb, fb), fb)
        h = h_ref[0, sl, :] + h_ref[1, sl, :]
        y = h * lax.rsqrt(jnp.mean(h * h, axis=-1, keepdims=True) + EPS)
        o_ref[sl, :] = (y * nw_ref[...] * og_ref[sl, :].astype(F32)).astype(BF16)
        return 0

    lax.fori_loop(0, S // fb, finish, 0)


def _mlstm(q, kt, v, og, rows, cols, norm_w, w_down, B, S):
    H = N_HEADS
    dk = q.shape[-1]
    dv = v.shape[-1]
    L = min(MLSTM_CHUNK, S)
    seq = lambda width: pl.BlockSpec((None, None, S, width), lambda b, h: (b, h, 0, 0))
    slab = _row_slabs(w_down, B * H, step=lambda b, h: b * H + h)
    kern = functools.partial(_mlstm_kernel, L=L)
    return pl.pallas_call(
        kern,
        grid=(B, H),
        in_specs=[seq(dk),
                  pl.BlockSpec((None, dk, S), lambda b, h: (b, h, 0)),
                  seq(dv), seq(dv),
                  pl.BlockSpec((None, None, S, cols.shape[-1]), lambda b, h: (b, h, 0, 0)),
                  pl.BlockSpec((None, None, N_ROWS, S), lambda b, h: (b, h, 0, 0)),
                  pl.BlockSpec((1, dv), lambda b, h: (0, h)),
                  slab],
        out_specs=[seq(dv), slab],
        out_shape=[jax.ShapeDtypeStruct((B, H, S, dv), BF16),
                   jax.ShapeDtypeStruct(w_down.shape, BF16)],
        scratch_shapes=[pltpu.VMEM((2, S, dv), F32),
                        pltpu.VMEM((2, dk, dv + DEN_LANES), F32)],
        compiler_params=_params(("arbitrary", "arbitrary")),
        name="mlstm",
    )(q, kt, v, og, cols, rows, norm_w.reshape(1, H * dv), w_down)


def _outproj_kernel(yf_ref, ym_ref, x_ref, mod_ref, nw_ref, w_ref, wu32_ref,
                    x1_ref, h2_ref, wu16_ref):
    d_f = yf_ref.shape[1]
    tm = x_ref.shape[0]
    n_heads, _, dv = ym_ref.shape
    _cast_up_slab(wu32_ref, wu16_ref)
    rc = min(256, tm)
    for r0 in range(0, tm, rc):
        rs = slice(r0, r0 + rc)
        proj = jnp.dot(yf_ref[rs, :], w_ref[0:d_f, :], preferred_element_type=F32)
        for h in range(n_heads):
            proj = proj + jnp.dot(ym_ref[h, rs, :], w_ref[d_f + h * dv:d_f + (h + 1) * dv, :],
                                  preferred_element_type=F32)
        x1 = x_ref[rs, :] + mod_ref[2:3, :] * proj
        x1_ref[rs, :] = x1
        h2_ref[rs, :] = _rms_mod(x1, nw_ref[...], mod_ref[3:4, :], mod_ref[4:5, :]).astype(BF16)


def _outproj(yf, ym, x2, mod, norm2_w, w_out, w_up, S):
    T, D = x2.shape
    d_f = yf.shape[1]
    _, H, _, dv = ym.shape
    tm = min(512, S)
    tiles_per_seq = S // tm
    n_steps = T // tm
    half = w_up.shape[0] // 2
    tok = lambda width: pl.BlockSpec((tm, width), lambda i: (i, 0))
    return pl.pallas_call(
        _outproj_kernel,
        grid=(n_steps,),
        in_specs=[tok(d_f),
                  pl.BlockSpec((None, H, tm, dv),
                               lambda i: (i // tiles_per_seq, 0, i % tiles_per_seq, 0)),
                  tok(D),
                  pl.BlockSpec((None, N_MOD, D), lambda i: (i // tiles_per_seq, 0, 0)),
                  pl.BlockSpec((1, D), lambda i: (0, 0)),
                  _resident((d_f + H * dv, D), lambda i: (0, 0)),
                  _row_slabs(w_up, n_steps, half, half)],
        out_specs=[tok(D), tok(D), _row_slabs(w_up, n_steps, 0, half)],
        out_shape=[jax.ShapeDtypeStruct((T, D), F32), jax.ShapeDtypeStruct((T, D), BF16),
                   jax.ShapeDtypeStruct((half, w_up.shape[1]), BF16)],
        compiler_params=_params(("arbitrary",)),
        name="outproj",
    )(yf, ym, x2, mod, norm2_w.reshape(1, D), w_out, w_up)


FFN_COLS = 256
FFN_SUB = 2
LANES = 128
CONV_PHASES = 4


def _interleave_gate_value(w, d_ff):
    lead = w.shape[:-1]
    nb = d_ff // FFN_COLS
    w = w.reshape(lead + (2, nb, FFN_COLS))
    return jnp.swapaxes(w, -3, -2).reshape(lead + (2 * d_ff,))


def _ffn_kernel(hp_ref, hc_ref, hn_ref, x1_ref, mod_ref, nfw_ref, wut_ref, wub_ref, cw_ref, cb_ref,
                wd_ref, o_ref, acc_ref, u_ref, y_ref, hext_ref, *, tiles_per_seq):
    i, j = pl.program_id(0), pl.program_id(1)
    tm = hc_ref.shape[0]

    @pl.when(j == 0)
    def _():
        acc_ref[...] = jnp.zeros_like(acc_ref)
        s_idx = i % tiles_per_seq
        hext_ref[0:HALO, :] = jnp.where(s_idx > 0, hp_ref[...], jnp.zeros_like(hp_ref))
        hext_ref[HALO:HALO + tm, :] = hc_ref[...]
        hext_ref[HALO + tm:, :] = jnp.where(s_idx < tiles_per_seq - 1, hn_ref[...],
                                            jnp.zeros_like(hn_ref))

    hext = hext_ref[...]

    w2 = 2 * FFN_COLS
    lw = u_ref.shape[-1]
    n_slab = w2 // lw
    rq = tm // CONV_PHASES
    kh = wut_ref.shape[0]
    for s in range(FFN_SUB):
        u = (jnp.dot(hext[:, :kh], wut_ref[:, s * w2:(s + 1) * w2], preferred_element_type=F32)
             + jnp.dot(hext[:, kh:], wub_ref[:, s * w2:(s + 1) * w2], preferred_element_type=F32))
        for l in range(n_slab):
            u_ref[s, l] = u[:, l * lw:(l + 1) * lw]
    for s in range(FFN_SUB):
        ys = []
        for l in range(n_slab):
            c0 = s * w2 + l * lw
            cw, cb = cw_ref[:, c0:c0 + lw], cb_ref[:, c0:c0 + lw]
            taps = [u_ref[s, l, pl.ds(HALO - 1 + m, rq, stride=CONV_PHASES), :]
                    for m in range(CONV_PHASES + 2)]
            ys.append(jnp.concatenate(
                [cw[0:1] * taps[k] + cw[1:2] * taps[k + 1] + cw[2:3] * taps[k + 2] + cb
                 for k in range(CONV_PHASES)], axis=0))
        y = jnp.concatenate(ys, axis=1)
        a, v = y[:, :FFN_COLS], y[:, FFN_COLS:]
        gact = (a * jax.nn.sigmoid(a) * v).astype(BF16)
        acc_ref[...] += jnp.dot(gact, wd_ref[s * FFN_COLS:(s + 1) * FFN_COLS, :],
                                preferred_element_type=F32)

    @pl.when(j == pl.num_programs(1) - 1)
    def _():
        D = o_ref.shape[1]
        sq = jnp.zeros((tm, lw), F32)
        for l in range(D // lw):
            cs = slice(l * lw, (l + 1) * lw)
            for k in range(CONV_PHASES):
                y_ref[l, pl.ds(k, rq, stride=CONV_PHASES), :] = acc_ref[k * rq:(k + 1) * rq, cs]
            y = x1_ref[:, cs] + mod_ref[5:6, cs] * y_ref[l]
            y_ref[l] = y
            sq = sq + y * y
        inv = lax.rsqrt(jnp.sum(sq, axis=-1, keepdims=True) / D + EPS)
        for l in range(D // lw):
            cs = slice(l * lw, (l + 1) * lw)
            o_ref[:, cs] = y_ref[l] * inv * nfw_ref[:, cs]


def _ffn(h2, x1, mod, norm_f_w, w_up_top, w_up_bot, w_conv, b_conv, w_down, S):
    T, D = x1.shape
    d_ff = w_down.shape[0]
    tm = min(512, S)
    tn = FFN_SUB * FFN_COLS
    tiles_per_seq = S // tm
    hb = tm // HALO
    last = T // HALO - 1
    kern = functools.partial(_ffn_kernel, tiles_per_seq=tiles_per_seq)
    return pl.pallas_call(
        kern,
        grid=(T // tm, d_ff // tn),
        in_specs=[pl.BlockSpec((HALO, D), lambda i, j: (jnp.maximum(i * hb - 1, 0), 0)),
                  pl.BlockSpec((tm, D), lambda i, j: (i, 0)),
                  pl.BlockSpec((HALO, D), lambda i, j: (jnp.minimum((i + 1) * hb, last), 0)),
                  pl.BlockSpec((tm, D), lambda i, j: (i, 0)),
                  pl.BlockSpec((None, N_MOD, D), lambda i, j: (i // tiles_per_seq, 0, 0)),
                  pl.BlockSpec((1, D), lambda i, j: (0, 0)),
                  pl.BlockSpec((w_up_top.shape[0], 2 * tn), lambda i, j: (0, j)),
                  pl.BlockSpec((w_up_bot.shape[0], 2 * tn), lambda i, j: (0, j)),
                  pl.BlockSpec((3, 2 * tn), lambda i, j: (0, j)),
                  pl.BlockSpec((1, 2 * tn), lambda i, j: (0, j)),
                  pl.BlockSpec((tn, D), lambda i, j: (j, 0))],
        out_specs=pl.BlockSpec((tm, D), lambda i, j: (i, 0)),
        out_shape=jax.ShapeDtypeStruct((T, D), F32),
        scratch_shapes=[pltpu.VMEM((tm, D), F32),
                        pltpu.VMEM((FFN_SUB, 2 * FFN_COLS // LANES, tm + 2 * HALO, LANES), F32),
                        pltpu.VMEM((D // LANES, tm, LANES), F32),
                        pltpu.VMEM((tm + 2 * HALO, D), BF16)],
        compiler_params=_params(("arbitrary", "arbitrary")),
        name="ffn",
    )(h2, h2, h2, x1, mod, norm_f_w.reshape(1, D), w_up_top, w_up_bot, w_conv, b_conv, w_down)


def kernel(x, c, w_ada, b_ada, norm1_w, w_in, b_gates, mlstm_norm_w, w_out, norm2_w, w_up,
           w_conv, b_conv, w_down, norm_f_w):
    B, S, D = x.shape
    d_f = D // 2
    d_m = D - d_f
    d_qk = d_m // 2
    dims = (d_f, d_qk, d_m)
    n_main = d_f + 2 * d_qk + 2 * d_m
    n_gates = 4 * N_HEADS
    assert w_in.shape[1] == n_main + n_gates and S % RADIX == 0

    x2 = x.reshape(B * S, D)
    w_main, w_gate = _cast_w_in(w_in.T, n_main, n_gates)
    b_gate = jnp.pad(b_gates, (0, N_GATE_LANES - n_gates)).reshape(N_GATE_LANES, 1)

    mod = _adaln(c, w_ada, b_ada)
    uf, q, kt, v, og, gates_t, w_out16, w_up16_top = _inproj(
        x2, mod, norm1_w, w_main, w_gate, b_gate, w_out, w_up, S, dims)
    yf = _fourier(uf, B, S)
    rows, cols = _gateprep(gates_t, B, S)
    ym, w_down16 = _mlstm(q, kt, v, og, rows, cols, mlstm_norm_w, w_down, B, S)
    x1, h2, w_up16_bot = _outproj(yf, ym, x2, mod, norm2_w, w_out16, w_up, S)
    d_ff = w_down.shape[0]
    out = _ffn(h2, x1, mod, norm_f_w, w_up16_top, w_up16_bot, _interleave_gate_value(w_conv, d_ff),
               _interleave_gate_value(b_conv, d_ff).reshape(1, -1), w_down16, S)
    return out.reshape(B, S, D)
```

```python
import functools

import numpy as np
import jax
import jax.numpy as jnp
from jax import lax
from jax.experimental import pallas as pl
from jax.experimental.pallas import tpu as pltpu

F32 = jnp.float32
BF16 = jnp.bfloat16

N_MOD = 6
N_FOURIER_GROUPS = 4
N_HEADS = 4
RADIX = 4
EPS = 1e-6
M_INIT = -1e30
MLSTM_CHUNK = 128
N_GATE_LANES = 128
HALO = 16
V7X_VMEM_LIMIT = 56 * 1024 * 1024


def _params(sem, vmem=V7X_VMEM_LIMIT, flags=None):
    return pltpu.CompilerParams(dimension_semantics=sem, vmem_limit_bytes=vmem, flags=flags)


def _resident(block_shape, index_map):
    return pl.BlockSpec(block_shape, index_map, pipeline_mode=pl.Buffered(1))


def _adaln_kernel(c_ref, w_ref, b_ref, o_ref):
    c = c_ref[...]
    s = c * jax.nn.sigmoid(c)
    s_hi = s.astype(BF16).astype(F32)
    row = lax.broadcasted_iota(jnp.int32, s.shape, 0)
    lhs = jnp.where(row < 8, s_hi, s - s_hi).astype(BF16)
    w = w_ref[...]
    w_hi = w.astype(BF16)
    w_lo = (w - w_hi.astype(F32)).astype(BF16)
    r = jnp.dot(lhs, w_hi, preferred_element_type=F32)
    r = r + jnp.dot(lhs, w_lo, preferred_element_type=F32)
    o_ref[...] = r[0:8] + r[8:16] + b_ref[...]


def _adaln(c, w_ada, b_ada):
    B, D = c.shape
    N = w_ada.shape[1]
    tn = min(1024, D)
    cp =jnp.zeros((16, D), F32).at[0:B].set(c).at[8:8 + B].set(c)
    out = pl.pallas_call(
        _adaln_kernel,
        grid=(N // tn,),
        in_specs=[pl.BlockSpec((16, D), lambda j: (0, 0)),
                  pl.BlockSpec((D, tn), lambda j: (0, j)),
                  pl.BlockSpec((1, tn), lambda j: (0, j))],
        out_specs=pl.BlockSpec((8, tn), lambda j: (0, j)),
        out_shape=jax.ShapeDtypeStruct((8, N), F32),
        compiler_params=_params(("arbitrary",)),
        name="adaln",
    )(cp, w_ada, b_ada.reshape(1, N))
    return out[:B].reshape(B, N_MOD, D)


def _rms_mod(x, nw, shift, scale):
    ms = jnp.mean(x * x, axis=-1, keepdims=True)
    return x * lax.rsqrt(ms + EPS) * (nw * (1.0 + scale)) + shift


def _cast_up_slab(src_ref, dst_ref):
    d_ff = src_ref.shape[1] // 2
    for g in range(d_ff // FFN_COLS):
        for half in range(2):
            src = half * d_ff + g * FFN_COLS
            dst = (2 * g + half) * FFN_COLS
            dst_ref[:, dst:dst + FFN_COLS] = src_ref[:, src:src + FFN_COLS].astype(BF16)


def _inproj_kernel(*refs, **static):
    first = pl.program_id(0) == 0
    pl.when(first)(functools.partial(_inproj_body, *refs, load_weights=True, **static))
    pl.when(jnp.logical_not(first))(functools.partial(_inproj_body, *refs, load_weights=False, **static))


def _inproj_body(x_ref, mod_ref, nw_ref, win_hbm, bg_ref, wo32_ref, wu32_ref,
                 uf_ref, q_ref, k_ref, v_ref, og_ref, g_ref, wo16_ref, wu16_ref,
                 uf_scr, w_ref, wg_ref, stage_ref, gstage_ref, sem,
                 *, dims, qscale, n_gates, load_weights):
    d_f, d_qk, d_m = dims
    n_main = w_ref.shape[0]
    tm = x_ref.shape[0]
    sr = stage_ref.shape[1]
    n_chunks = n_main // sr

    def chunk_copy(k):
        return pltpu.make_async_copy(win_hbm.at[pl.ds(k * sr, sr), :], stage_ref.at[k % 2], sem.at[k % 2])

    def gate_copy():
        return pltpu.make_async_copy(win_hbm.at[pl.ds(n_main, n_gates), :], gstage_ref, sem.at[2])

    n_loaded = [0]

    def need(c0, width):
        if not load_weights:
            return
        upto = -(-(c0 + width) // sr)
        for k in range(n_loaded[0], upto):
            chunk_copy(k).wait()
            if k + 1 < n_chunks:
                chunk_copy(k + 1).start()
            w_ref[k * sr:(k + 1) * sr, :] = stage_ref[k % 2].astype(BF16)
        n_loaded[0] = max(n_loaded[0], upto)

    if load_weights:
        gate_copy().start()
        chunk_copy(0).start()
    wo16_ref[...] = wo32_ref[...].astype(BF16)
    _cast_up_slab(wu32_ref, wu16_ref)
    h = _rms_mod(x_ref[...], nw_ref[...], mod_ref[0:1, :], mod_ref[1:2, :]).astype(BF16)

    nt = (((1,), (1,)), ((), ()))

    def proj(c0, width):
        need(c0, width)
        return lax.dot_general(h, w_ref[c0:c0 + width, :], nt, preferred_element_type=F32)

    cw = min(512, d_qk)
    lw = uf_scr.shape[-1]
    for c0 in range(0, d_f, cw):
        u = proj(c0, cw)
        for l in range(cw // lw):
            uf_scr[l] = u[:, l * lw:(l + 1) * lw]
        for l in range(cw // lw):
            for r in range(RADIX):
                col = r * d_f + c0 + l * lw
                uf_ref[:, col:col + lw] = (
                    uf_scr[l, pl.ds(r, tm // RADIX, stride=RADIX), :].astype(BF16))
    def put_heads(ref, c0, val):
        dh = ref.shape[-1]
        for c in range(0, val.shape[1], dh):
            ref[(c0 + c) // dh] = val[:, c:c + dh]

    o = d_f
    for c0 in range(0, d_qk, cw):
        put_heads(q_ref, c0, (proj(o + c0, cw) * qscale).astype(BF16))
    o += d_qk
    for c0 in range(0, d_qk, cw):
        need(o + c0, cw)
        k_ref[c0:c0 + cw, :] = lax.dot_general(w_ref[o + c0:o + c0 + cw, :], h, nt,
                                               preferred_element_type=F32).astype(BF16)
    o += d_qk
    for c0 in range(0, d_m, cw):
        put_heads(v_ref, c0, proj(o + c0, cw).astype(BF16))
    o += d_m
    for c0 in range(0, d_m, cw):
        put_heads(og_ref, c0, jax.nn.sigmoid(proj(o + c0, cw)).astype(BF16))
    if load_weights:
        gate_copy().wait()
        wg_ref[...] = jnp.zeros_like(wg_ref)
        wg_ref[0:n_gates, :] = gstage_ref[...].astype(BF16)
    g_ref[...] = lax.dot_general(wg_ref[...], h, nt, preferred_element_type=F32) + bg_ref[...]


def _row_slabs(w, n_steps, first_row=0, n_rows=None, step=lambda *ids: ids[0]):
    n_rows = w.shape[0] - first_row if n_rows is None else n_rows
    rows = n_rows // n_steps
    assert rows * n_steps == n_rows and rows % 16 == 0 and first_row % rows == 0
    return pl.BlockSpec((rows, w.shape[1]), lambda *ids: (step(*ids) + first_row // rows, 0))


W_STAGE_ROWS = 256


def _inproj(x2, mod, norm1_w, w_in_t, b_gate, w_out, w_up, S, dims):
    T, D = x2.shape
    d_f, d_qk, d_m = dims
    tm = min(512, S)
    tiles_per_seq = S // tm
    n_main = d_f + 2 * d_qk + 2 * d_m
    n_gates = w_in_t.shape[0] - n_main
    sr = min(W_STAGE_ROWS, n_main)
    assert n_main % sr == 0 and n_gates % 16 == 0 and n_gates <= N_GATE_LANES
    n_steps = T // tm
    tok = lambda width: pl.BlockSpec((tm, width), lambda i: (i, 0))
    tok_t = lambda rows: pl.BlockSpec((None, rows, tm),
                                      lambda i: (i // tiles_per_seq, 0, i % tiles_per_seq))
    H = N_HEADS
    tok_h = lambda dh: pl.BlockSpec((None, H, tm, dh),
                                    lambda i: (i // tiles_per_seq, 0, i % tiles_per_seq, 0))
    kern = functools.partial(_inproj_kernel, dims=dims, n_gates=n_gates,
                             qscale=float((d_qk // N_HEADS) ** -0.5))
    return pl.pallas_call(
        kern,
        grid=(T // tm,),
        in_specs=[tok(D),
                  pl.BlockSpec((None, N_MOD, D), lambda i: (i // tiles_per_seq, 0, 0)),
                  pl.BlockSpec((1, D), lambda i: (0, 0)),
                  pl.BlockSpec(memory_space=pl.ANY),
                  pl.BlockSpec((N_GATE_LANES, 1), lambda i: (0, 0)),
                  _row_slabs(w_out, n_steps), _row_slabs(w_up, n_steps, 0, w_up.shape[0] // 2)],
        out_specs=[pl.BlockSpec((tm // RADIX, RADIX * d_f), lambda i: (i, 0)),
                   tok_h(d_qk // H), tok_t(d_qk), tok_h(d_m // H), tok_h(d_m // H),
                   tok_t(N_GATE_LANES),
                   _row_slabs(w_out, n_steps), _row_slabs(w_up, n_steps, 0, w_up.shape[0] // 2)],
        out_shape=[jax.ShapeDtypeStruct((T // RADIX, RADIX * d_f), BF16),
                   jax.ShapeDtypeStruct((T // S, H, S, d_qk // H), BF16),
                   jax.ShapeDtypeStruct((T // S, d_qk, S), BF16),
                   jax.ShapeDtypeStruct((T // S, H, S, d_m // H), BF16),
                   jax.ShapeDtypeStruct((T // S, H, S, d_m // H), BF16),
                   jax.ShapeDtypeStruct((T // S, N_GATE_LANES, S), F32),
                   jax.ShapeDtypeStruct(w_out.shape, BF16),
                   jax.ShapeDtypeStruct((w_up.shape[0] // 2, w_up.shape[1]), BF16)],
        scratch_shapes=[pltpu.VMEM((max(1, min(512, d_qk) // 128), tm, min(128, d_qk)), F32),
                        pltpu.VMEM((n_main, D), BF16),
                        pltpu.VMEM((N_GATE_LANES, D), BF16),
                        pltpu.VMEM((2, sr, D), F32),
                        pltpu.VMEM((n_gates, D), F32),
                        pltpu.SemaphoreType.DMA((3,))],
        compiler_params=_params(("arbitrary",)),
        name="inproj",
    )(x2, mod, norm1_w.reshape(1, D), w_in_t, b_gate, w_out, w_up)


def _dft_constants(S, gc, tk):
    n1 = S // RADIX
    idx = np.arange(n1, dtype=np.int64)
    ang = 2.0 * np.pi * ((idx[:, None] * idx[None, :]) % n1).astype(np.float64) / n1
    c, s = np.cos(ang), np.sin(ang)
    nt = n1 // tk
    cs = np.concatenate([c.reshape(nt, tk, n1), s.reshape(nt, tk, n1)], axis=1)
    tw = []
    for r in range(1, RADIX):
        a = 2.0 * np.pi * ((r * idx) % S).astype(np.float64) / S
        tw += [np.cos(a), np.sin(a)]
    tw = np.broadcast_to(np.stack(tw)[:, :, None], (2 * (RADIX - 1), n1, 128))
    ch = np.arange(gc, dtype=np.int64)
    angc = 2.0 * np.pi * ((ch[:, None] * ch[None, :]) % gc).astype(np.float64) / gc
    wch = np.concatenate([np.cos(angc), np.sin(angc)], axis=0) / np.sqrt(float(S) * gc)
    return (jnp.asarray(cs, dtype=BF16), jnp.asarray(np.ascontiguousarray(tw), dtype=F32),
            jnp.asarray(wch, dtype=BF16))


def _fourier_kernel(x_ref, cs_ref, tw_ref, wch_ref, o_ref, *, gc, n_groups, tk):
    n1 = x_ref.shape[0]
    d_f = gc * n_groups
    cs = cs_ref[...]
    wch = wch_ref[...]
    reps = gc // 128 if gc >= 128 else 1

    def lanes(t):
        if gc < 128:
            return t[:, :gc]
        return jnp.concatenate([t] * reps, axis=1) if reps > 1 else t

    c1, s1, c2, s2, c3, s3 = [lanes(tw_ref[i]) for i in range(2 * (RADIX - 1))]
    for g in range(n_groups):
        zc, zs = [], []
        for r in range(RADIX):
            z = jnp.dot(cs, x_ref[:, r * d_f + g * gc:r * d_f + (g + 1) * gc],
                        preferred_element_type=F32)
            zc.append(z[:tk])
            zs.append(z[tk:])
        t0r, t0i = zc[0], -zs[0]
        t1r, t1i = c1 * zc[1] - s1 * zs[1], -(c1 * zs[1] + s1 * zc[1])
        t2r, t2i = c2 * zc[2] - s2 * zs[2], -(c2 * zs[2] + s2 * zc[2])
        t3r, t3i = c3 * zc[3] - s3 * zs[3], -(c3 * zs[3] + s3 * zc[3])
        ar, ai = t0r + t2r, t0i + t2i
        br, bi = t0r - t2r, t0i - t2i
        cr, ci = t1r + t3r, t1i + t3i
        dr, di = t1r - t3r, t1i - t3i
        xq = [(ar + cr, ai + ci), (br + di, bi - dr), (ar - cr, ai - ci), (br - di, bi + dr)]
        for q in range(RADIX):
            xre, xim = xq[q]
            lhs = jnp.concatenate([xre.astype(BF16), xim.astype(BF16)], axis=1)
            y = jnp.dot(lhs, wch, preferred_element_type=F32)
            o_ref[q, :, g * gc:(g + 1) * gc] = y.astype(BF16)


def _fourier(uf4, B, S):
    d_f = uf4.shape[-1] // RADIX
    gc = d_f // N_FOURIER_GROUPS
    n1 = S // RADIX
    tk = min(512, n1)
    cs, tw, wch = _dft_constants(S, gc, tk)
    x4 = uf4.reshape(B, n1, RADIX * d_f)
    kern = functools.partial(_fourier_kernel, gc=gc, n_groups=N_FOURIER_GROUPS, tk=tk)
    out = pl.pallas_call(
        kern,
        grid=(B, n1 // tk),
        in_specs=[pl.BlockSpec((None, n1, RADIX * d_f), lambda b, t: (b, 0, 0)),
                  pl.BlockSpec((None, 2 * tk, n1), lambda b, t: (t, 0, 0)),
                  pl.BlockSpec((2 * (RADIX - 1), tk, 128), lambda b, t: (0, t, 0)),
                  pl.BlockSpec((2 * gc, gc), lambda b, t: (0, 0))],
        out_specs=pl.BlockSpec((None, RADIX, tk, d_f), lambda b, t: (b, 0, t, 0)),
        out_shape=jax.ShapeDtypeStruct((B, RADIX, n1, d_f), BF16),
        compiler_params=_params(("arbitrary", "arbitrary")),
        name="fourier",
    )(x4, cs, tw, wch)
    return out.reshape(B * S, d_f)


_ROW_R, _ROW_WK, _ROW_SOLD = 0, 2, 4
_ROW_COLS = 8
_COL_A, _COL_SINT, _COL_EML = 0, 1, 2
N_ROWS = 16


def _lane_scan(x, op, identity, reverse):
    n = x.shape[-1]
    lane = lax.broadcasted_iota(jnp.int32, x.shape, x.ndim - 1)
    shift = 1
    while shift < n:
        if reverse:
            rolled = pltpu.roll(x, n - shift, axis=x.ndim - 1)
            x = op(x, jnp.where(lane < n - shift, rolled, identity))
        else:
            rolled = pltpu.roll(x, shift, axis=x.ndim - 1)
            x = op(x, jnp.where(lane >= shift, rolled, identity))
        shift *= 2
    return x


def _gateprep_kernel(g_ref, o_ref, mst_ref, mnx_ref):
    B, H = o_ref.shape[0], o_ref.shape[1]
    nc, L = g_ref.shape[2], g_ref.shape[3]
    for d in range(2):
        reverse = d == 1
        rows = lambda first: jnp.concatenate(
            [g_ref[b, (2 * d + first) * H:(2 * d + first + 1) * H] for b in range(B)], axis=0)
        ig = rows(0)
        lf = jax.nn.log_sigmoid(rows(1))
        b_cum = _lane_scan(lf, jnp.add, 0.0, reverse)
        g = b_cum[:, :, 0:1] if reverse else b_cum[:, :, L - 1:L]
        r = ig - b_cum
        cm = _lane_scan(r, jnp.maximum, -jnp.inf, reverse)
        m_loc = g + jnp.max(r, axis=-1, keepdims=True)
        m = jnp.full((B * H, 1, 1), M_INIT, F32)
        for step in range(nc):
            c = nc - 1 - step if reverse else step
            mst_ref[:, c:c + 1, :] = m
            m = jnp.maximum(g[:, c:c + 1, :] + m, m_loc[:, c:c + 1, :])
            mnx_ref[:, c:c + 1, :] = m
        m_start, m_next = mst_ref[...], mnx_ref[...]
        top = jnp.maximum(cm, m_start)
        vals = {_ROW_R + d: r,
                _ROW_WK + d: jnp.exp(g + r - m_next),
                _ROW_SOLD + d: jnp.broadcast_to(jnp.exp(g + m_start - m_next), r.shape),
                _ROW_COLS + 4 * d + _COL_A: -top,
                _ROW_COLS + 4 * d + _COL_SINT: jnp.exp(m_start - top),
                _ROW_COLS + 4 * d + _COL_EML: jnp.exp(-(b_cum + top))}
        for row, val in vals.items():
            for b in range(B):
                for h in range(H):
                    o_ref[b, h, row] = val[b * H + h]
    zero = jnp.zeros((nc, L), F32)
    for row in (6, 7, _ROW_COLS + 3, _ROW_COLS + 7):
        for b in range(B):
            for h in range(H):
                o_ref[b, h, row] = zero


def _gateprep(gates_t, B, S):
    H, L = N_HEADS, min(MLSTM_CHUNK, S)
    nc = S // L
    rows = pl.pallas_call(
        _gateprep_kernel,
        out_shape=jax.ShapeDtypeStruct((B, H, N_ROWS, nc, L), F32),
        scratch_shapes=[pltpu.VMEM((B * H, nc, 1), F32), pltpu.VMEM((B * H, nc, 1), F32)],
        name="gateprep",
    )(gates_t[:, :4 * H].reshape(B, 4 * H, nc, L))
    rows = rows.reshape(B, H, N_ROWS, S)
    cols = jnp.swapaxes(rows[:, :, _ROW_COLS:, :], 2, 3)
    return rows, cols


DEN_LANES = 128


def _lane_tile(x, width):
    n = x.shape[1]
    if width % n == 0:
        return jnp.concatenate([x] * (width // n), axis=1) if width > n else x
    return jnp.broadcast_to(x[:, 0:1], (x.shape[0], width))


def _mlstm_kernel(q_ref, kt_ref, v_ref, og_ref, cols_ref, rows_ref, nw_ref, wd32_ref,
                  o_ref, wd16_ref, h_ref, c_ref, *, L):
    S, dk = q_ref.shape
    dv = v_ref.shape[1]
    nc = S // L
    jj = lax.broadcasted_iota(jnp.int32, (L, L), 0)
    tt = lax.broadcasted_iota(jnp.int32, (L, L), 1)
    ones = jnp.ones((L, DEN_LANES), BF16)
    c_ref[...] = jnp.zeros_like(c_ref)
    wd16_ref[...] = wd32_ref[...].astype(BF16)

    def step(i, _):
        dirs = (0, 1)
        sls = [pl.ds(pl.multiple_of(c * L, L), L) for c in (i, nc - 1 - i)]
        row = lambda d, k: rows_ref[k + d:k + d + 1, sls[d]]
        col = lambda d, k: cols_ref[sls[d], 4 * d + k:4 * d + k + 1]
        qc = [q_ref[sls[d], :] for d in dirs]
        ktc = [kt_ref[:, sls[d]] for d in dirs]
        vaug = [jnp.concatenate([v_ref[sls[d], :], ones], axis=1) for d in dirs]
        state = [c_ref[d] for d in dirs]
        s = [jnp.dot(qc[d], ktc[d], preferred_element_type=F32) for d in dirs]
        kwt = [(ktc[d].astype(F32) * row(d, _ROW_WK)).astype(BF16) for d in dirs]
        upd = [jnp.dot(kwt[d], vaug[d], preferred_element_type=F32) for d in dirs]
        for d in dirs:
            decay = _lane_tile(jnp.broadcast_to(row(d, _ROW_SOLD), (dk, L)), dv + DEN_LANES)
            c_ref[d] = decay * state[d] + upd[d]
        tot = []
        for d in dirs:
            allowed = (tt <= jj) if d == 0 else (tt >= jj)
            p = jnp.where(allowed, jnp.exp(col(d, _COL_A) + row(d, _ROW_R)), 0.0) * s[d]
            lhs = jnp.concatenate(
                [p.astype(BF16), (qc[d].astype(F32) * col(d, _COL_SINT)).astype(BF16)], axis=1)
            rhs = jnp.concatenate([vaug[d], state[d].astype(BF16)], axis=0)
            tot.append(jnp.dot(lhs, rhs, preferred_element_type=F32))
        for d in dirs:
            inv = 1.0 / jnp.maximum(jnp.abs(tot[d][:, dv:]), col(d, _COL_EML))
            h_ref[d, sls[d], :] = tot[d][:, :dv] * _lane_tile(inv, dv)
        return 0

    lax.fori_loop(0, nc, step, 0, unroll=4)

    fb = min(S, 4 * L)

    def finish(i, _):
        sl = pl.ds(pl.multiple_of(i * fb, fb), fb)
        h = h_ref[0, sl, :] + h_ref[1, sl, :]
        y = h * lax.rsqrt(jnp.mean(h * h, axis=-1, keepdims=True) + EPS)
        o_ref[sl, :] = (y * nw_ref[...] * og_ref[sl, :].astype(F32)).astype(BF16)
        return 0

    lax.fori_loop(0, S // fb, finish, 0)


def _mlstm(q, kt, v, og, rows, cols, norm_w, w_down, B, S):
    H = N_HEADS
    dk = q.shape[-1]
    dv = v.shape[-1]
    L = min(MLSTM_CHUNK, S)
    seq = lambda width: pl.BlockSpec((None, None, S, width), lambda b, h: (b, h, 0, 0))
    slab = _row_slabs(w_down, B * H, step=lambda b, h: b * H + h)
    kern = functools.partial(_mlstm_kernel, L=L)
    return pl.pallas_call(
        kern,
        grid=(B, H),
        in_specs=[seq(dk),
                  pl.BlockSpec((None, dk, S), lambda b, h: (b, h, 0)),
                  seq(dv), seq(dv),
                  pl.BlockSpec((None, None, S, cols.shape[-1]), lambda b, h: (b, h, 0, 0)),
                  pl.BlockSpec((None, None, N_ROWS, S), lambda b, h: (b, h, 0, 0)),
                  pl.BlockSpec((1, dv), lambda b, h: (0, h)),
                  slab],
        out_specs=[seq(dv), slab],
        out_shape=[jax.ShapeDtypeStruct((B, H, S, dv), BF16),
                   jax.ShapeDtypeStruct(w_down.shape, BF16)],
        scratch_shapes=[pltpu.VMEM((2, S, dv), F32),
                        pltpu.VMEM((2, dk, dv + DEN_LANES), F32)],
        compiler_params=_params(("arbitrary", "arbitrary")),
        name="mlstm",
    )(q, kt, v, og, cols, rows, norm_w.reshape(1, H * dv), w_down)


def _outproj_kernel(yf_ref, ym_ref, x_ref, mod_ref, nw_ref, w_ref, wu32_ref,
                    x1_ref, h2_ref, wu16_ref):
    d_f = yf_ref.shape[1]
    tm = x_ref.shape[0]
    n_heads, _, dv = ym_ref.shape
    _cast_up_slab(wu32_ref, wu16_ref)
    rc = min(256, tm)
    for r0 in range(0, tm, rc):
        rs = slice(r0, r0 + rc)
        proj = jnp.dot(yf_ref[rs, :], w_ref[0:d_f, :], preferred_element_type=F32)
        for h in range(n_heads):
            proj = proj + jnp.dot(ym_ref[h, rs, :], w_ref[d_f + h * dv:d_f + (h + 1) * dv, :],
                                  preferred_element_type=F32)
        x1 = x_ref[rs, :] + mod_ref[2:3, :] * proj
        x1_ref[rs, :] = x1
        h2_ref[rs, :] = _rms_mod(x1, nw_ref[...], mod_ref[3:4, :], mod_ref[4:5, :]).astype(BF16)


def _outproj(yf, ym, x2, mod, norm2_w, w_out, w_up, S):
    T, D = x2.shape
    d_f = yf.shape[1]
    _, H, _, dv = ym.shape
    tm = min(512, S)
    tiles_per_seq = S // tm
    n_steps = T // tm
    half = w_up.shape[0] // 2
    tok = lambda width: pl.BlockSpec((tm, width), lambda i: (i, 0))
    return pl.pallas_call(
        _outproj_kernel,
        grid=(n_steps,),
        in_specs=[tok(d_f),
                  pl.BlockSpec((None, H, tm, dv),
                               lambda i: (i // tiles_per_seq, 0, i % tiles_per_seq, 0)),
                  tok(D),
                  pl.BlockSpec((None, N_MOD, D), lambda i: (i // tiles_per_seq, 0, 0)),
                  pl.BlockSpec((1, D), lambda i: (0, 0)),
                  _resident((d_f + H * dv, D), lambda i: (0, 0)),
                  _row_slabs(w_up, n_steps, half, half)],
        out_specs=[tok(D), tok(D), _row_slabs(w_up, n_steps, 0, half)],
        out_shape=[jax.ShapeDtypeStruct((T, D), F32), jax.ShapeDtypeStruct((T, D), BF16),
                   jax.ShapeDtypeStruct((half, w_up.shape[1]), BF16)],
        compiler_params=_params(("arbitrary",)),
        name="outproj",
    )(yf, ym, x2, mod, norm2_w.reshape(1, D), w_out, w_up)


FFN_COLS = 256
FFN_SUB = 2
LANES = 128
CONV_PHASES = 4


def _interleave_gate_value(w, d_ff):
    lead = w.shape[:-1]
    nb = d_ff // FFN_COLS
    w = w.reshape(lead + (2, nb, FFN_COLS))
    return jnp.swapaxes(w, -3, -2).reshape(lead + (2 * d_ff,))


def _ffn_kernel(hp_ref, hc_ref, hn_ref, x1_ref, mod_ref, nfw_ref, wut_ref, wub_ref, cw_ref, cb_ref,
                wd_ref, o_ref, acc_ref, u_ref, y_ref, hext_ref, *, tiles_per_seq):
    i, j = pl.program_id(0), pl.program_id(1)
    tm = hc_ref.shape[0]

    @pl.when(j == 0)
    def _():
        acc_ref[...] = jnp.zeros_like(acc_ref)
        s_idx = i % tiles_per_seq
        hext_ref[0:HALO, :] = jnp.where(s_idx > 0, hp_ref[...], jnp.zeros_like(hp_ref))
        hext_ref[HALO:HALO + tm, :] = hc_ref[...]
        hext_ref[HALO + tm:, :] = jnp.where(s_idx < tiles_per_seq - 1, hn_ref[...],
                                            jnp.zeros_like(hn_ref))

    hext = hext_ref[...]

    w2 = 2 * FFN_COLS
    lw = u_ref.shape[-1]
    n_slab = w2 // lw
    rq = tm // CONV_PHASES
    kh = wut_ref.shape[0]
    for s in range(FFN_SUB):
        u = (jnp.dot(hext[:, :kh], wut_ref[:, s * w2:(s + 1) * w2], preferred_element_type=F32)
             + jnp.dot(hext[:, kh:], wub_ref[:, s * w2:(s + 1) * w2], preferred_element_type=F32))
        for l in range(n_slab):
            u_ref[s, l] = u[:, l * lw:(l + 1) * lw]
    for s in range(FFN_SUB):
        ys = []
        for l in range(n_slab):
            c0 = s * w2 + l * lw
            cw, cb = cw_ref[:, c0:c0 + lw], cb_ref[:, c0:c0 + lw]
            taps = [u_ref[s, l, pl.ds(HALO - 1 + m, rq, stride=CONV_PHASES), :]
                    for m in range(CONV_PHASES + 2)]
            ys.append(jnp.concatenate(
                [cw[0:1] * taps[k] + cw[1:2] * taps[k + 1] + cw[2:3] * taps[k + 2] + cb
                 for k in range(CONV_PHASES)], axis=0))
        y = jnp.concatenate(ys, axis=1)
        a, v = y[:, :FFN_COLS], y[:, FFN_COLS:]
        gact = (a * jax.nn.sigmoid(a) * v).astype(BF16)
        acc_ref[...] += jnp.dot(gact, wd_ref[s * FFN_COLS:(s + 1) * FFN_COLS, :],
                                preferred_element_type=F32)

    @pl.when(j == pl.num_programs(1) - 1)
    def _():
        D = o_ref.shape[1]
        sq = jnp.zeros((tm, lw), F32)
        for l in range(D // lw):
            cs = slice(l * lw, (l + 1) * lw)
            for k in range(CONV_PHASES):
                y_ref[l, pl.ds(k, rq, stride=CONV_PHASES), :] = acc_ref[k * rq:(k + 1) * rq, cs]
            y = x1_ref[:, cs] + mod_ref[5:6, cs] * y_ref[l]
            y_ref[l] = y
            sq = sq + y * y
        inv = lax.rsqrt(jnp.sum(sq, axis=-1, keepdims=True) / D + EPS)
        for l in range(D // lw):
            cs = slice(l * lw, (l + 1) * lw)
            o_ref[:, cs] = y_ref[l] * inv * nfw_ref[:, cs]


def _ffn(h2, x1, mod, norm_f_w, w_up_top, w_up_bot, w_conv, b_conv, w_down, S):
    T, D = x1.shape
    d_ff = w_down.shape[0]
    tm = min(512, S)
    tn = FFN_SUB * FFN_COLS
    tiles_per_seq = S // tm
    hb = tm // HALO
    last = T // HALO - 1
    kern = functools.partial(_ffn_kernel, tiles_per_seq=tiles_per_seq)
    return pl.pallas_call(
        kern,
        grid=(T // tm, d_ff // tn),
        in_specs=[pl.BlockSpec((HALO, D), lambda i, j: (jnp.maximum(i * hb - 1, 0), 0)),
                  pl.BlockSpec((tm, D), lambda i, j: (i, 0)),
                  pl.BlockSpec((HALO, D), lambda i, j: (jnp.minimum((i + 1) * hb, last), 0)),
                  pl.BlockSpec((tm, D), lambda i, j: (i, 0)),
                  pl.BlockSpec((None, N_MOD, D), lambda i, j: (i // tiles_per_seq, 0, 0)),
                  pl.BlockSpec((1, D), lambda i, j: (0, 0)),
                  pl.BlockSpec((w_up_top.shape[0], 2 * tn), lambda i, j: (0, j)),
                  pl.BlockSpec((w_up_bot.shape[0], 2 * tn), lambda i, j: (0, j)),
                  pl.BlockSpec((3, 2 * tn), lambda i, j: (0, j)),
                  pl.BlockSpec((1, 2 * tn), lambda i, j: (0, j)),
                  pl.BlockSpec((tn, D), lambda i, j: (j, 0))],
        out_specs=pl.BlockSpec((tm, D), lambda i, j: (i, 0)),
        out_shape=jax.ShapeDtypeStruct((T, D), F32),
        scratch_shapes=[pltpu.VMEM((tm, D), F32),
                        pltpu.VMEM((FFN_SUB, 2 * FFN_COLS // LANES, tm + 2 * HALO, LANES), F32),
                        pltpu.VMEM((D // LANES, tm, LANES), F32),
                        pltpu.VMEM((tm + 2 * HALO, D), BF16)],
        compiler_params=_params(("arbitrary", "arbitrary")),
        name="ffn",
    )(h2, h2, h2, x1, mod, norm_f_w.reshape(1, D), w_up_top, w_up_bot, w_conv, b_conv, w_down)


def kernel(x, c, w_ada, b_ada, norm1_w, w_in, b_gates, mlstm_norm_w, w_out, norm2_w, w_up,
           w_conv, b_conv, w_down, norm_f_w):
    B, S, D = x.shape
    d_f = D // 2
    d_m = D - d_f
    d_qk = d_m // 2
    dims = (d_f, d_qk, d_m)
    n_main = d_f + 2 * d_qk + 2 * d_m
    n_gates = 4 * N_HEADS
    assert w_in.shape[1] == n_main + n_gates and S % RADIX == 0

    x2 = x.reshape(B * S, D)
    b_gate = jnp.pad(b_gates, (0, N_GATE_LANES - n_gates)).reshape(N_GATE_LANES, 1)

    mod = _adaln(c, w_ada, b_ada)
    uf, q, kt, v, og, gates_t, w_out16, w_up16_top = _inproj(
        x2, mod, norm1_w, w_in.T, b_gate, w_out, w_up, S, dims)
    yf = _fourier(uf, B, S)
    rows, cols = _gateprep(gates_t, B, S)
    ym, w_down16 = _mlstm(q, kt, v, og, rows, cols, mlstm_norm_w, w_down, B, S)
    x1, h2, w_up16_bot = _outproj(yf, ym, x2, mod, norm2_w, w_out16, w_up, S)
    d_ff = w_down.shape[0]
    out = _ffn(h2, x1, mod, norm_f_w, w_up16_top, w_up16_bot, _interleave_gate_value(w_conv, d_ff),
               _interleave_gate_value(b_conv, d_ff).reshape(1, -1), w_down16, S)
    return out.reshape(B, S, D)
```

```python
import functools

import numpy as np
import jax
import jax.numpy as jnp
from jax import lax
from jax.experimental import pallas as pl
from jax.experimental.pallas import tpu as pltpu

F32 = jnp.float32
BF16 = jnp.bfloat16

N_MOD = 6
N_FOURIER_GROUPS = 4
N_HEADS = 4
RADIX = 4
EPS = 1e-6
M_INIT = -1e30
MLSTM_CHUNK = 128
LANES = 128
SUBLANES = 8
N_GATE_LANES = LANES
HALO = 2 * SUBLANES
V7X_VMEM_LIMIT = 56 * 1024 * 1024


def _params(sem, vmem=V7X_VMEM_LIMIT, flags=None):
    return pltpu.CompilerParams(dimension_semantics=sem, vmem_limit_bytes=vmem, flags=flags)


def _resident(block_shape, index_map):
    return pl.BlockSpec(block_shape, index_map, pipeline_mode=pl.Buffered(1))


def _adaln_kernel(c_ref, w_ref, b_ref, o_ref):
    c = c_ref[...]
    s = c * jax.nn.sigmoid(c)
    s_hi = s.astype(BF16).astype(F32)
    row = lax.broadcasted_iota(jnp.int32, s.shape, 0)
    lhs = jnp.where(row < SUBLANES, s_hi, s - s_hi).astype(BF16)
    w = w_ref[...]
    w_hi = w.astype(BF16)
    w_lo = (w - w_hi.astype(F32)).astype(BF16)
    r = jnp.dot(lhs, w_hi, preferred_element_type=F32)
    r = r + jnp.dot(lhs, w_lo, preferred_element_type=F32)
    o_ref[...] = r[0:SUBLANES] + r[SUBLANES:2 * SUBLANES] + b_ref[...]


def _adaln(c, w_ada, b_ada):
    B, D = c.shape
    N = w_ada.shape[1]
    assert B <= SUBLANES
    tn = min(2048, D)
    cp = jnp.zeros((2 * SUBLANES, D), F32).at[0:B].set(c).at[SUBLANES:SUBLANES + B].set(c)
    out = pl.pallas_call(
        _adaln_kernel,
        grid=(N // tn,),
        in_specs=[pl.BlockSpec((2 * SUBLANES, D), lambda j: (0, 0)),
                  pl.BlockSpec((D, tn), lambda j: (0, j)),
                  pl.BlockSpec((1, tn), lambda j: (0, j))],
        out_specs=pl.BlockSpec((SUBLANES, tn), lambda j: (0, j)),
        out_shape=jax.ShapeDtypeStruct((SUBLANES, N), F32),
        compiler_params=_params(("arbitrary",)),
        name="adaln",
    )(cp, w_ada, b_ada.reshape(1, N))
    return out[:B].reshape(B, N_MOD, D)


def _rms_mod(x, nw, shift, scale):
    ms = jnp.mean(x * x, axis=-1, keepdims=True)
    return x * lax.rsqrt(ms + EPS) * (nw * (1.0 + scale)) + shift


def _cast_up_slab(src_ref, dst_ref):
    d_ff = src_ref.shape[1] // 2
    for g in range(d_ff // FFN_COLS):
        for half in range(2):
            src = half * d_ff + g * FFN_COLS
            dst = (2 * g + half) * FFN_COLS
            dst_ref[:, dst:dst + FFN_COLS] = src_ref[:, src:src + FFN_COLS].astype(BF16)


def _inproj_kernel(*refs, **static):
    first = pl.program_id(0) == 0
    pl.when(first)(functools.partial(_inproj_body, *refs, load_weights=True, **static))
    pl.when(jnp.logical_not(first))(functools.partial(_inproj_body, *refs, load_weights=False, **static))


def _inproj_body(x_ref, mod_ref, nw_ref, win_hbm, bg_ref, wo32_ref, wu32_ref,
                 uf_ref, q_ref, k_ref, v_ref, og_ref, g_ref, wo16_ref, wu16_ref,
                 uf_scr, w_ref, wg_ref, stage_ref, gstage_ref, sem,
                 *, dims, qscale, n_gates, load_weights):
    d_f, d_qk, d_m = dims
    n_main = w_ref.shape[0]
    tm = x_ref.shape[0]
    n_slots, sr = stage_ref.shape[0], stage_ref.shape[1]
    n_chunks = n_main // sr
    ahead = n_slots - 1

    def chunk_copy(k):
        slot = k % n_slots
        return pltpu.make_async_copy(win_hbm.at[pl.ds(k * sr, sr), :], stage_ref.at[slot], sem.at[slot])

    def gate_copy():
        return pltpu.make_async_copy(win_hbm.at[pl.ds(n_main, n_gates), :], gstage_ref, sem.at[n_slots])

    n_loaded = [0]

    def need(c0, width):
        if not load_weights:
            return
        upto = -(-(c0 + width) // sr)
        for k in range(n_loaded[0], upto):
            chunk_copy(k).wait()
            if k + ahead < n_chunks:
                chunk_copy(k + ahead).start()
            w_ref[k * sr:(k + 1) * sr, :] = stage_ref[k % n_slots].astype(BF16)
        n_loaded[0] = max(n_loaded[0], upto)

    if load_weights:
        gate_copy().start()
        for k in range(min(ahead, n_chunks)):
            chunk_copy(k).start()
    wo16_ref[...] = wo32_ref[...].astype(BF16)
    _cast_up_slab(wu32_ref, wu16_ref)
    h = _rms_mod(x_ref[...], nw_ref[...], mod_ref[0:1, :], mod_ref[1:2, :]).astype(BF16)

    nt = (((1,), (1,)), ((), ()))

    def proj(c0, width):
        need(c0, width)
        return lax.dot_general(h, w_ref[c0:c0 + width, :], nt, preferred_element_type=F32)

    cw = min(512, d_qk)
    lw = uf_scr.shape[-1]
    for c0 in range(0, d_f, cw):
        u = proj(c0, cw)
        for l in range(cw // lw):
            uf_scr[l] = u[:, l * lw:(l + 1) * lw]
        for l in range(cw // lw):
            for r in range(RADIX):
                col = r * d_f + c0 + l * lw
                uf_ref[:, col:col + lw] = (
                    uf_scr[l, pl.ds(r, tm // RADIX, stride=RADIX), :].astype(BF16))
    def put_heads(ref, c0, val):
        dh = ref.shape[-1]
        for c in range(0, val.shape[1], dh):
            ref[(c0 + c) // dh] = val[:, c:c + dh]

    o = d_f
    for c0 in range(0, d_qk, cw):
        put_heads(q_ref, c0, (proj(o + c0, cw) * qscale).astype(BF16))
    o += d_qk
    for c0 in range(0, d_qk, cw):
        need(o + c0, cw)
        k_ref[c0:c0 + cw, :] = lax.dot_general(w_ref[o + c0:o + c0 + cw, :], h, nt,
                                               preferred_element_type=F32).astype(BF16)
    o += d_qk
    for c0 in range(0, d_m, cw):
        put_heads(v_ref, c0, proj(o + c0, cw).astype(BF16))
    o += d_m
    for c0 in range(0, d_m, cw):
        put_heads(og_ref, c0, jax.nn.sigmoid(proj(o + c0, cw)).astype(BF16))
    if load_weights:
        gate_copy().wait()
        wg_ref[...] = jnp.zeros_like(wg_ref)
        wg_ref[0:n_gates, :] = gstage_ref[...].astype(BF16)
    g_ref[...] = lax.dot_general(wg_ref[...], h, nt, preferred_element_type=F32) + bg_ref[...]


def _row_slabs(w, n_steps, first_row=0, n_rows=None, step=lambda *ids: ids[0]):
    n_rows = w.shape[0] - first_row if n_rows is None else n_rows
    rows = n_rows // n_steps
    assert rows * n_steps == n_rows and rows % 16 == 0 and first_row % rows == 0
    return pl.BlockSpec((rows, w.shape[1]), lambda *ids: (step(*ids) + first_row // rows, 0))


W_STAGE_ROWS = 128
W_STAGE_SLOTS = 4


def _inproj(x2, mod, norm1_w, w_in_t, b_gate, w_out, w_up, S, dims):
    T, D = x2.shape
    d_f, d_qk, d_m = dims
    tm = min(512, S)
    tiles_per_seq = S // tm
    n_main = d_f + 2 * d_qk + 2 * d_m
    n_gates = w_in_t.shape[0] - n_main
    sr = min(W_STAGE_ROWS, n_main)
    assert n_main % sr == 0 and n_gates % 16 == 0 and n_gates <= N_GATE_LANES
    n_steps = T // tm
    tok = lambda width: pl.BlockSpec((tm, width), lambda i: (i, 0))
    tok_t = lambda rows: pl.BlockSpec((None, rows, tm),
                                      lambda i: (i // tiles_per_seq, 0, i % tiles_per_seq))
    H = N_HEADS
    tok_h = lambda dh: pl.BlockSpec((None, H, tm, dh),
                                    lambda i: (i // tiles_per_seq, 0, i % tiles_per_seq, 0))
    kern = functools.partial(_inproj_kernel, dims=dims, n_gates=n_gates,
                             qscale=float((d_qk // N_HEADS) ** -0.5))
    return pl.pallas_call(
        kern,
        grid=(T // tm,),
        in_specs=[tok(D),
                  pl.BlockSpec((None, N_MOD, D), lambda i: (i // tiles_per_seq, 0, 0)),
                  pl.BlockSpec((1, D), lambda i: (0, 0)),
                  pl.BlockSpec(memory_space=pl.ANY),
                  pl.BlockSpec((N_GATE_LANES, 1), lambda i: (0, 0)),
                  _row_slabs(w_out, n_steps), _row_slabs(w_up, n_steps, 0, w_up.shape[0] // 2)],
        out_specs=[pl.BlockSpec((tm // RADIX, RADIX * d_f), lambda i: (i, 0)),
                   tok_h(d_qk // H), tok_t(d_qk), tok_h(d_m // H), tok_h(d_m // H),
                   tok_t(N_GATE_LANES),
                   _row_slabs(w_out, n_steps), _row_slabs(w_up, n_steps, 0, w_up.shape[0] // 2)],
        out_shape=[jax.ShapeDtypeStruct((T // RADIX, RADIX * d_f), BF16),
                   jax.ShapeDtypeStruct((T // S, H, S, d_qk // H), BF16),
                   jax.ShapeDtypeStruct((T // S, d_qk, S), BF16),
                   jax.ShapeDtypeStruct((T // S, H, S, d_m // H), BF16),
                   jax.ShapeDtypeStruct((T // S, H, S, d_m // H), BF16),
                   jax.ShapeDtypeStruct((T // S, N_GATE_LANES, S), F32),
                   jax.ShapeDtypeStruct(w_out.shape, BF16),
                   jax.ShapeDtypeStruct((w_up.shape[0] // 2, w_up.shape[1]), BF16)],
        scratch_shapes=[pltpu.VMEM((max(1, min(512, d_qk) // LANES), tm, min(LANES, d_qk)), F32),
                        pltpu.VMEM((n_main, D), BF16),
                        pltpu.VMEM((N_GATE_LANES, D), BF16),
                        pltpu.VMEM((W_STAGE_SLOTS, sr, D), F32),
                        pltpu.VMEM((n_gates, D), F32),
                        pltpu.SemaphoreType.DMA((W_STAGE_SLOTS + 1,))],
        compiler_params=_params(("arbitrary",)),
        name="inproj",
    )(x2, mod, norm1_w.reshape(1, D), w_in_t, b_gate, w_out, w_up)


def _dft_constants(S, gc, tk):
    n1 = S // RADIX
    idx = np.arange(n1, dtype=np.int64)
    ang = 2.0 * np.pi * ((idx[:, None] * idx[None, :]) % n1).astype(np.float64) / n1
    c, s = np.cos(ang), np.sin(ang)
    nt = n1 // tk
    cs = np.concatenate([c.reshape(nt, tk, n1), s.reshape(nt, tk, n1)], axis=1)
    tw = []
    for r in range(1, RADIX):
        a = 2.0 * np.pi * ((r * idx) % S).astype(np.float64) / S
        tw += [np.cos(a), np.sin(a)]
    tw = np.broadcast_to(np.stack(tw)[:, :, None], (2 * (RADIX - 1), n1, LANES))
    ch = np.arange(gc, dtype=np.int64)
    angc = 2.0 * np.pi * ((ch[:, None] * ch[None, :]) % gc).astype(np.float64) / gc
    wch = np.concatenate([np.cos(angc), np.sin(angc)], axis=0) / np.sqrt(float(S) * gc)
    return (jnp.asarray(cs, dtype=BF16), jnp.asarray(np.ascontiguousarray(tw), dtype=F32),
            jnp.asarray(wch, dtype=BF16))


def _fourier_kernel(x_ref, cs_ref, tw_ref, wch_ref, wu32_ref, o_ref, wu16_ref, *, gc, n_groups, tk):
    n1 = x_ref.shape[0]
    _cast_up_slab(wu32_ref, wu16_ref)
    d_f = gc * n_groups
    cs = cs_ref[...]
    wch = wch_ref[...]
    reps = gc // LANES if gc >= LANES else 1

    def lanes(t):
        if gc < LANES:
            return t[:, :gc]
        return jnp.concatenate([t] * reps, axis=1) if reps > 1 else t

    c1, s1, c2, s2, c3, s3 = [lanes(tw_ref[i]) for i in range(2 * (RADIX - 1))]

    def sub_dfts(g):
        return [jnp.dot(cs, x_ref[:, r * d_f + g * gc:r * d_f + (g + 1) * gc],
                        preferred_element_type=F32) for r in range(RADIX)]

    z_next = sub_dfts(0)
    for g in range(n_groups):
        z = z_next
        if g + 1 < n_groups:
            z_next = sub_dfts(g + 1)
        zc, zs = [t[:tk] for t in z], [t[tk:] for t in z]
        t0r, t0i = zc[0], -zs[0]
        t1r, t1i = c1 * zc[1] - s1 * zs[1], -(c1 * zs[1] + s1 * zc[1])
        t2r, t2i = c2 * zc[2] - s2 * zs[2], -(c2 * zs[2] + s2 * zc[2])
        t3r, t3i = c3 * zc[3] - s3 * zs[3], -(c3 * zs[3] + s3 * zc[3])
        ar, ai = t0r + t2r, t0i + t2i
        br, bi = t0r - t2r, t0i - t2i
        cr, ci = t1r + t3r, t1i + t3i
        dr, di = t1r - t3r, t1i - t3i
        xq = [(ar + cr, ai + ci), (br + di, bi - dr), (ar - cr, ai - ci), (br - di, bi + dr)]
        for q in range(RADIX):
            xre, xim = xq[q]
            lhs = jnp.concatenate([xre.astype(BF16), xim.astype(BF16)], axis=1)
            y = jnp.dot(lhs, wch, preferred_element_type=F32)
            o_ref[q, :, g * gc:(g + 1) * gc] = y.astype(BF16)


def _fourier(uf4, w_up, B, S):
    d_f = uf4.shape[-1] // RADIX
    gc = d_f // N_FOURIER_GROUPS
    n1 = S // RADIX
    tk = min(256, n1)
    nt = n1 // tk
    half = w_up.shape[0] // 2
    slab = lambda first: _row_slabs(w_up, B * nt, first, half, step=lambda b, t: b * nt + t)
    cs, tw, wch = _dft_constants(S, gc, tk)
    x4 = uf4.reshape(B, n1, RADIX * d_f)
    kern = functools.partial(_fourier_kernel, gc=gc, n_groups=N_FOURIER_GROUPS, tk=tk)
    out, w_up16_bot = pl.pallas_call(
        kern,
        grid=(B, nt),
        in_specs=[pl.BlockSpec((None, n1, RADIX * d_f), lambda b, t: (b, 0, 0)),
                  pl.BlockSpec((None, 2 * tk, n1), lambda b, t: (t, 0, 0)),
                  pl.BlockSpec((2 * (RADIX - 1), tk, LANES), lambda b, t: (0, t, 0)),
                  pl.BlockSpec((2 * gc, gc), lambda b, t: (0, 0)),
                  slab(half)],
        out_specs=[pl.BlockSpec((None, RADIX, tk, d_f), lambda b, t: (b, 0, t, 0)), slab(0)],
        out_shape=[jax.ShapeDtypeStruct((B, RADIX, n1, d_f), BF16),
                   jax.ShapeDtypeStruct((half, w_up.shape[1]), BF16)],
        compiler_params=_params(("arbitrary", "arbitrary")),
        name="fourier",
    )(x4, cs, tw, wch, w_up)
    return out.reshape(B * S, d_f), w_up16_bot


_ROW_R, _ROW_WK, _ROW_SOLD = 0, 2, 4
_ROW_COLS = 8
_COL_A, _COL_SINT, _COL_EML = 0, 1, 2
N_ROWS = 16


def _lane_scan(x, op, identity, reverse):
    n = x.shape[-1]
    lane = lax.broadcasted_iota(jnp.int32, x.shape, x.ndim - 1)
    shift = 1
    while shift < n:
        if reverse:
            rolled = pltpu.roll(x, n - shift, axis=x.ndim - 1)
            x = op(x, jnp.where(lane < n - shift, rolled, identity))
        else:
            rolled = pltpu.roll(x, shift, axis=x.ndim - 1)
            x = op(x, jnp.where(lane >= shift, rolled, identity))
        shift *= 2
    return x


def _gateprep_kernel(g_ref, o_ref, mst_ref, mnx_ref):
    B, H = o_ref.shape[0], o_ref.shape[1]
    nc, L = g_ref.shape[2], g_ref.shape[3]
    for d in range(2):
        reverse = d == 1
        rows = lambda first: jnp.concatenate(
            [g_ref[b, (2 * d + first) * H:(2 * d + first + 1) * H] for b in range(B)], axis=0)
        ig = rows(0)
        lf = jax.nn.log_sigmoid(rows(1))
        b_cum = _lane_scan(lf, jnp.add, 0.0, reverse)
        g = b_cum[:, :, 0:1] if reverse else b_cum[:, :, L - 1:L]
        r = ig - b_cum
        cm = _lane_scan(r, jnp.maximum, -jnp.inf, reverse)
        m_loc = g + jnp.max(r, axis=-1, keepdims=True)
        m = jnp.full((B * H, 1, 1), M_INIT, F32)
        for step in range(nc):
            c = nc - 1 - step if reverse else step
            mst_ref[:, c:c + 1, :] = m
            m = jnp.maximum(g[:, c:c + 1, :] + m, m_loc[:, c:c + 1, :])
            mnx_ref[:, c:c + 1, :] = m
        m_start, m_next = mst_ref[...], mnx_ref[...]
        top = jnp.maximum(cm, m_start)
        vals = {_ROW_R + d: r,
                _ROW_WK + d: jnp.exp(g + r - m_next),
                _ROW_SOLD + d: jnp.broadcast_to(jnp.exp(g + m_start - m_next), r.shape),
                _ROW_COLS + 4 * d + _COL_A: -top,
                _ROW_COLS + 4 * d + _COL_SINT: jnp.exp(m_start - top),
                _ROW_COLS + 4 * d + _COL_EML: jnp.exp(-(b_cum + top))}
        for row, val in vals.items():
            for b in range(B):
                for h in range(H):
                    o_ref[b, h, row] = val[b * H + h]
    zero = jnp.zeros((nc, L), F32)
    for row in (6, 7, _ROW_COLS + 3, _ROW_COLS + 7):
        for b in range(B):
            for h in range(H):
                o_ref[b, h, row] = zero


def _gateprep(gates_t, B, S):
    H, L = N_HEADS, min(MLSTM_CHUNK, S)
    nc = S // L
    rows = pl.pallas_call(
        _gateprep_kernel,
        out_shape=jax.ShapeDtypeStruct((B, H, N_ROWS, nc, L), F32),
        scratch_shapes=[pltpu.VMEM((B * H, nc, 1), F32), pltpu.VMEM((B * H, nc, 1), F32)],
        name="gateprep",
    )(gates_t[:, :4 * H].reshape(B, 4 * H, nc, L))
    rows = rows.reshape(B, H, N_ROWS, S)
    cols = jnp.swapaxes(rows[:, :, _ROW_COLS:, :], 2, 3)
    return rows, cols


DEN_LANES = LANES


def _lane_tile(x, width):
    n = x.shape[1]
    if width % n == 0:
        return jnp.concatenate([x] * (width // n), axis=1) if width > n else x
    return jnp.broadcast_to(x[:, 0:1], (x.shape[0], width))


def _mlstm_kernel(q_ref, kt_ref, v_ref, og_ref, cols_ref, rows_ref, nw_ref, wd32_ref,
                  o_ref, wd16_ref, h_ref, c_ref, *, L):
    S, dk = q_ref.shape
    dv = v_ref.shape[1]
    nc = S // L
    jj = lax.broadcasted_iota(jnp.int32, (L, L), 0)
    tt = lax.broadcasted_iota(jnp.int32, (L, L), 1)
    ones = jnp.ones((L, DEN_LANES), BF16)
    c_ref[...] = jnp.zeros_like(c_ref)
    wd16_ref[...] = wd32_ref[...].astype(BF16)

    def step(i, _):
        dirs = (0, 1)
        sls = [pl.ds(pl.multiple_of(c * L, L), L) for c in (i, nc - 1 - i)]
        row = lambda d, k: rows_ref[k + d:k + d + 1, sls[d]]
        col = lambda d, k: cols_ref[sls[d], 4 * d + k:4 * d + k + 1]
        qc = [q_ref[sls[d], :] for d in dirs]
        ktc = [kt_ref[:, sls[d]] for d in dirs]
        vaug = [jnp.concatenate([v_ref[sls[d], :], ones], axis=1) for d in dirs]
        state = [c_ref[d] for d in dirs]
        s = [jnp.dot(qc[d], ktc[d], preferred_element_type=F32) for d in dirs]
        kwt = [(ktc[d].astype(F32) * row(d, _ROW_WK)).astype(BF16) for d in dirs]
        upd = [jnp.dot(kwt[d], vaug[d], preferred_element_type=F32) for d in dirs]
        for d in dirs:
            decay = _lane_tile(jnp.broadcast_to(row(d, _ROW_SOLD), (dk, L)), dv + DEN_LANES)
            c_ref[d] = decay * state[d] + upd[d]
        tot = []
        for d in dirs:
            allowed = (tt <= jj) if d == 0 else (tt >= jj)
            p = jnp.where(allowed, jnp.exp(col(d, _COL_A) + row(d, _ROW_R)), 0.0) * s[d]
            lhs = jnp.concatenate(
                [p.astype(BF16), (qc[d].astype(F32) * col(d, _COL_SINT)).astype(BF16)], axis=1)
            rhs = jnp.concatenate([vaug[d], state[d].astype(BF16)], axis=0)
            tot.append(jnp.dot(lhs, rhs, preferred_element_type=F32))
        for d in dirs:
            inv = 1.0 / jnp.maximum(jnp.abs(tot[d][:, dv:]), col(d, _COL_EML))
            h_ref[d, sls[d], :] = tot[d][:, :dv] * _lane_tile(inv, dv)
        return 0

    lax.fori_loop(0, nc, step, 0, unroll=8)

    fb = min(S, 4 * L)

    def finish(i, _):
        sl = pl.ds(pl.multiple_of(i * fb, fb), fb)
        h = h_ref[0, sl, :] + h_ref[1, sl, :]
        y = h * lax.rsqrt(jnp.mean(h * h, axis=-1, keepdims=True) + EPS)
        o_ref[sl, :] = (y * nw_ref[...] * og_ref[sl, :].astype(F32)).astype(BF16)
        return 0

    lax.fori_loop(0, S // fb, finish, 0)


def _mlstm(q, kt, v, og, rows, cols, norm_w, w_down, B, S):
    H = N_HEADS
    dk = q.shape[-1]
    dv = v.shape[-1]
    L = min(MLSTM_CHUNK, S)
    seq = lambda width: pl.BlockSpec((None, None, S, width), lambda b, h: (b, h, 0, 0))
    slab = _row_slabs(w_down, B * H, step=lambda b, h: b * H + h)
    kern = functools.partial(_mlstm_kernel, L=L)
    return pl.pallas_call(
        kern,
        grid=(B, H),
        in_specs=[seq(dk),
                  pl.BlockSpec((None, dk, S), lambda b, h: (b, h, 0)),
                  seq(dv), seq(dv),
                  pl.BlockSpec((None, None, S, cols.shape[-1]), lambda b, h: (b, h, 0, 0)),
                  pl.BlockSpec((None, None, N_ROWS, S), lambda b, h: (b, h, 0, 0)),
                  pl.BlockSpec((1, dv), lambda b, h: (0, h)),
                  slab],
        out_specs=[seq(dv), slab],
        out_shape=[jax.ShapeDtypeStruct((B, H, S, dv), BF16),
                   jax.ShapeDtypeStruct(w_down.shape, BF16)],
        scratch_shapes=[pltpu.VMEM((2, S, dv), F32),
                        pltpu.VMEM((2, dk, dv + DEN_LANES), F32)],
        compiler_params=_params(("arbitrary", "arbitrary")),
        name="mlstm",
    )(q, kt, v, og, cols, rows, norm_w.reshape(1, H * dv), w_down)


def _outproj_kernel(yf_ref, ym_ref, x_ref, mod_ref, nw_ref, w_ref, x1_ref, h2_ref):
    d_f = yf_ref.shape[1]
    tm = x_ref.shape[0]
    n_heads, _, dv = ym_ref.shape
    rc = min(256, tm)
    for r0 in range(0, tm, rc):
        rs = slice(r0, r0 + rc)
        proj = jnp.dot(yf_ref[rs, :], w_ref[0:d_f, :], preferred_element_type=F32)
        for h in range(n_heads):
            proj = proj + jnp.dot(ym_ref[h, rs, :], w_ref[d_f + h * dv:d_f + (h + 1) * dv, :],
                                  preferred_element_type=F32)
        x1 = x_ref[rs, :] + mod_ref[2:3, :] * proj
        x1_ref[rs, :] = x1
        h2_ref[rs, :] = _rms_mod(x1, nw_ref[...], mod_ref[3:4, :], mod_ref[4:5, :]).astype(BF16)


def _outproj(yf, ym, x2, mod, norm2_w, w_out, S):
    T, D = x2.shape
    d_f = yf.shape[1]
    _, H, _, dv = ym.shape
    tm = min(512, S)
    tiles_per_seq = S // tm
    n_steps = T // tm
    tok = lambda width: pl.BlockSpec((tm, width), lambda i: (i, 0))
    return pl.pallas_call(
        _outproj_kernel,
        grid=(n_steps,),
        in_specs=[tok(d_f),
                  pl.BlockSpec((None, H, tm, dv),
                               lambda i: (i // tiles_per_seq, 0, i % tiles_per_seq, 0)),
                  tok(D),
                  pl.BlockSpec((None, N_MOD, D), lambda i: (i // tiles_per_seq, 0, 0)),
                  pl.BlockSpec((1, D), lambda i: (0, 0)),
                  _resident((d_f + H * dv, D), lambda i: (0, 0))],
        out_specs=[tok(D), tok(D)],
        out_shape=[jax.ShapeDtypeStruct((T, D), F32), jax.ShapeDtypeStruct((T, D), BF16)],
        compiler_params=_params(("arbitrary",)),
        name="outproj",
    )(yf, ym, x2, mod, norm2_w.reshape(1, D), w_out)


FFN_COLS = 256
FFN_SUB = 2
CONV_PHASES = 4


def _interleave_gate_value(w, d_ff):
    lead = w.shape[:-1]
    nb = d_ff // FFN_COLS
    w = w.reshape(lead + (2, nb, FFN_COLS))
    return jnp.swapaxes(w, -3, -2).reshape(lead + (2 * d_ff,))


def _ffn_kernel(hp_ref, hc_ref, hn_ref, x1_ref, mod_ref, nfw_ref, wut_ref, wub_ref, cw_ref, cb_ref,
                wd_ref, o_ref, acc_ref, u_ref, y_ref, hext_ref, *, tiles_per_seq):
    i, j = pl.program_id(0), pl.program_id(1)
    tm = hc_ref.shape[0]

    @pl.when(j == 0)
    def _():
        acc_ref[...] = jnp.zeros_like(acc_ref)
        s_idx = i % tiles_per_seq
        hext_ref[0:HALO, :] = jnp.where(s_idx > 0, hp_ref[...], jnp.zeros_like(hp_ref))
        hext_ref[HALO:HALO + tm, :] = hc_ref[...]
        hext_ref[HALO + tm:, :] = jnp.where(s_idx < tiles_per_seq - 1, hn_ref[...],
                                            jnp.zeros_like(hn_ref))

    hext = hext_ref[...]

    w2 = 2 * FFN_COLS
    lw = u_ref.shape[-1]
    n_slab = w2 // lw
    rq = tm // CONV_PHASES
    kh = wut_ref.shape[0]
    for s in range(FFN_SUB):
        u = (jnp.dot(hext[:, :kh], wut_ref[:, s * w2:(s + 1) * w2], preferred_element_type=F32)
             + jnp.dot(hext[:, kh:], wub_ref[:, s * w2:(s + 1) * w2], preferred_element_type=F32))
        for l in range(n_slab):
            u_ref[s, l] = u[:, l * lw:(l + 1) * lw]
    for s in range(FFN_SUB):
        ys = []
        for l in range(n_slab):
            c0 = s * w2 + l * lw
            cw, cb = cw_ref[:, c0:c0 + lw], cb_ref[:, c0:c0 + lw]
            taps = [u_ref[s, l, pl.ds(HALO - 1 + m, rq, stride=CONV_PHASES), :]
                    for m in range(CONV_PHASES + 2)]
            ys.append(jnp.concatenate(
                [cw[0:1] * taps[k] + cw[1:2] * taps[k + 1] + cw[2:3] * taps[k + 2] + cb
                 for k in range(CONV_PHASES)], axis=0))
        y = jnp.concatenate(ys, axis=1)
        a, v = y[:, :FFN_COLS], y[:, FFN_COLS:]
        gact = (a * jax.nn.sigmoid(a) * v).astype(BF16)
        acc_ref[...] += jnp.dot(gact, wd_ref[s * FFN_COLS:(s + 1) * FFN_COLS, :],
                                preferred_element_type=F32)

    @pl.when(j == pl.num_programs(1) - 1)
    def _():
        D = o_ref.shape[1]
        sq = jnp.zeros((tm, lw), F32)
        for l in range(D // lw):
            cs = slice(l * lw, (l + 1) * lw)
            for k in range(CONV_PHASES):
                y_ref[l, pl.ds(k, rq, stride=CONV_PHASES), :] = acc_ref[k * rq:(k + 1) * rq, cs]
            y = x1_ref[:, cs] + mod_ref[5:6, cs] * y_ref[l]
            y_ref[l] = y
            sq = sq + y * y
        inv = lax.rsqrt(jnp.sum(sq, axis=-1, keepdims=True) / D + EPS)
        for l in range(D // lw):
            cs = slice(l * lw, (l + 1) * lw)
            o_ref[:, cs] = y_ref[l] * inv * nfw_ref[:, cs]


def _ffn(h2, x1, mod, norm_f_w, w_up_top, w_up_bot, w_conv, b_conv, w_down, S):
    T, D = x1.shape
    d_ff = w_down.shape[0]
    tm = min(512, S)
    tn = FFN_SUB * FFN_COLS
    tiles_per_seq = S // tm
    hb = tm // HALO
    last = T // HALO - 1
    kern = functools.partial(_ffn_kernel, tiles_per_seq=tiles_per_seq)
    return pl.pallas_call(
        kern,
        grid=(T // tm, d_ff // tn),
        in_specs=[pl.BlockSpec((HALO, D), lambda i, j: (jnp.maximum(i * hb - 1, 0), 0)),
                  pl.BlockSpec((tm, D), lambda i, j: (i, 0)),
                  pl.BlockSpec((HALO, D), lambda i, j: (jnp.minimum((i + 1) * hb, last), 0)),
                  pl.BlockSpec((tm, D), lambda i, j: (i, 0)),
                  pl.BlockSpec((None, N_MOD, D), lambda i, j: (i // tiles_per_seq, 0, 0)),
                  pl.BlockSpec((1, D), lambda i, j: (0, 0)),
                  pl.BlockSpec((w_up_top.shape[0], 2 * tn), lambda i, j: (0, j)),
                  pl.BlockSpec((w_up_bot.shape[0], 2 * tn), lambda i, j: (0, j)),
                  pl.BlockSpec((3, 2 * tn), lambda i, j: (0, j)),
                  pl.BlockSpec((1, 2 * tn), lambda i, j: (0, j)),
                  pl.BlockSpec((tn, D), lambda i, j: (j, 0))],
        out_specs=pl.BlockSpec((tm, D), lambda i, j: (i, 0)),
        out_shape=jax.ShapeDtypeStruct((T, D), F32),
        scratch_shapes=[pltpu.VMEM((tm, D), F32),
                        pltpu.VMEM((FFN_SUB, 2 * FFN_COLS // LANES, tm + 2 * HALO, LANES), F32),
                        pltpu.VMEM((D // LANES, tm, LANES), F32),
                        pltpu.VMEM((tm + 2 * HALO, D), BF16)],
        compiler_params=_params(("arbitrary", "arbitrary")),
        name="ffn",
    )(h2, h2, h2, x1, mod, norm_f_w.reshape(1, D), w_up_top, w_up_bot, w_conv, b_conv, w_down)


def kernel(x, c, w_ada, b_ada, norm1_w, w_in, b_gates, mlstm_norm_w, w_out, norm2_w, w_up,
           w_conv, b_conv, w_down, norm_f_w):
    B, S, D = x.shape
    d_f = D // 2
    d_m = D - d_f
    d_qk = d_m // 2
    dims = (d_f, d_qk, d_m)
    n_main = d_f + 2 * d_qk + 2 * d_m
    n_gates = 4 * N_HEADS
    assert w_in.shape[1] == n_main + n_gates and S % RADIX == 0

    x2 = x.reshape(B * S, D)
    b_gate = jnp.pad(b_gates, (0, N_GATE_LANES - n_gates)).reshape(N_GATE_LANES, 1)

    mod = _adaln(c, w_ada, b_ada)
    uf, q, kt, v, og, gates_t, w_out16, w_up16_top = _inproj(
        x2, mod, norm1_w, w_in.T, b_gate, w_out, w_up, S, dims)
    yf, w_up16_bot = _fourier(uf, w_up, B, S)
    rows, cols = _gateprep(gates_t, B, S)
    ym, w_down16 = _mlstm(q, kt, v, og, rows, cols, mlstm_norm_w, w_down, B, S)
    x1, h2 = _outproj(yf, ym, x2, mod, norm2_w, w_out16, S)
    d_ff = w_down.shape[0]
    out = _ffn(h2, x1, mod, norm_f_w, w_up16_top, w_up16_bot, _interleave_gate_value(w_conv, d_ff),
               _interleave_gate_value(b_conv, d_ff).reshape(1, -1), w_down16, S)
    return out.reshape(B, S, D)
```

```python
import functools

import numpy as np
import jax
import jax.numpy as jnp
from jax import lax
from jax.experimental import pallas as pl
from jax.experimental.pallas import tpu as pltpu

F32 = jnp.float32
BF16 = jnp.bfloat16

N_MOD = 6
N_FOURIER_GROUPS = 4
N_HEADS = 4
RADIX = 4
EPS = 1e-6
M_INIT = -1e30
MLSTM_CHUNK = 128
LANES = 128
SUBLANES = 8
N_GATE_LANES = LANES
HALO = 2 * SUBLANES
V7X_VMEM_LIMIT = 56 * 1024 * 1024


def _params(sem, vmem=V7X_VMEM_LIMIT, flags=None):
    return pltpu.CompilerParams(dimension_semantics=sem, vmem_limit_bytes=vmem, flags=flags)


def _resident(block_shape, index_map):
    return pl.BlockSpec(block_shape, index_map, pipeline_mode=pl.Buffered(1))


def _adaln_kernel(c_ref, w_ref, b_ref, o_ref):
    c = c_ref[...]
    s = c * jax.nn.sigmoid(c)
    s_hi = s.astype(BF16).astype(F32)
    row = lax.broadcasted_iota(jnp.int32, s.shape, 0)
    lhs = jnp.where(row < SUBLANES, s_hi, s - s_hi).astype(BF16)
    w = w_ref[...]
    w_hi = w.astype(BF16)
    w_lo = (w - w_hi.astype(F32)).astype(BF16)
    r = jnp.dot(lhs, w_hi, preferred_element_type=F32)
    r = r + jnp.dot(lhs, w_lo, preferred_element_type=F32)
    o_ref[...] = r[0:SUBLANES] + r[SUBLANES:2 * SUBLANES] + b_ref[...]


def _adaln(c, w_ada, b_ada):
    B, D = c.shape
    N = w_ada.shape[1]
    assert B <= SUBLANES
    tn = min(2048, D)
    cp = jnp.zeros((2 * SUBLANES, D), F32).at[0:B].set(c).at[SUBLANES:SUBLANES + B].set(c)
    out = pl.pallas_call(
        _adaln_kernel,
        grid=(N // tn,),
        in_specs=[pl.BlockSpec((2 * SUBLANES, D), lambda j: (0, 0)),
                  pl.BlockSpec((D, tn), lambda j: (0, j)),
                  pl.BlockSpec((1, tn), lambda j: (0, j))],
        out_specs=pl.BlockSpec((SUBLANES, tn), lambda j: (0, j)),
        out_shape=jax.ShapeDtypeStruct((SUBLANES, N), F32),
        compiler_params=_params(("arbitrary",)),
        name="adaln",
    )(cp, w_ada, b_ada.reshape(1, N))
    return out[:B].reshape(B, N_MOD, D)


def _rms_mod(x, nw, shift, scale):
    ms = jnp.mean(x * x, axis=-1, keepdims=True)
    return x * lax.rsqrt(ms + EPS) * (nw * (1.0 + scale)) + shift


def _cast_up_slab(src_ref, dst_ref):
    d_ff = src_ref.shape[1] // 2
    for g in range(d_ff // FFN_COLS):
        for half in range(2):
            src = half * d_ff + g * FFN_COLS
            dst = (2 * g + half) * FFN_COLS
            dst_ref[:, dst:dst + FFN_COLS] = src_ref[:, src:src + FFN_COLS].astype(BF16)


def _inproj_kernel(*refs, **static):
    first = pl.program_id(0) == 0
    pl.when(first)(functools.partial(_inproj_body, *refs, load_weights=True, **static))
    pl.when(jnp.logical_not(first))(functools.partial(_inproj_body, *refs, load_weights=False, **static))


def _inproj_body(x_ref, mod_ref, nw_ref, win_hbm, bg_ref, wo32_ref, wu32_ref,
                 uf_ref, q_ref, k_ref, v_ref, og_ref, g_ref, wo16_ref, wu16_ref,
                 uf_scr, w_ref, wg_ref, stage_ref, gstage_ref, sem,
                 *, dims, qscale, n_gates, load_weights):
    d_f, d_qk, d_m = dims
    n_main = w_ref.shape[0]
    tm = x_ref.shape[0]
    n_slots, sr = stage_ref.shape[0], stage_ref.shape[1]
    n_chunks = n_main // sr
    ahead = n_slots - 1

    def chunk_copy(k):
        slot = k % n_slots
        return pltpu.make_async_copy(win_hbm.at[pl.ds(k * sr, sr), :], stage_ref.at[slot], sem.at[slot])

    def gate_copy():
        return pltpu.make_async_copy(win_hbm.at[pl.ds(n_main, n_gates), :], gstage_ref, sem.at[n_slots])

    n_loaded = [0]

    def need(c0, width):
        if not load_weights:
            return
        upto = -(-(c0 + width) // sr)
        for k in range(n_loaded[0], upto):
            chunk_copy(k).wait()
            if k + ahead < n_chunks:
                chunk_copy(k + ahead).start()
            w_ref[k * sr:(k + 1) * sr, :] = stage_ref[k % n_slots].astype(BF16)
        n_loaded[0] = max(n_loaded[0], upto)

    if load_weights:
        gate_copy().start()
        for k in range(min(ahead, n_chunks)):
            chunk_copy(k).start()
    wo16_ref[...] = wo32_ref[...].astype(BF16)
    _cast_up_slab(wu32_ref, wu16_ref)
    h = _rms_mod(x_ref[...], nw_ref[...], mod_ref[0:1, :], mod_ref[1:2, :]).astype(BF16)

    nt = (((1,), (1,)), ((), ()))

    def proj(c0, width):
        need(c0, width)
        return lax.dot_general(h, w_ref[c0:c0 + width, :], nt, preferred_element_type=F32)

    cw = min(512, d_qk)
    lw = uf_scr.shape[-1]
    for c0 in range(0, d_f, cw):
        u = proj(c0, cw)
        for l in range(cw // lw):
            uf_scr[l] = u[:, l * lw:(l + 1) * lw]
        for l in range(cw // lw):
            for r in range(RADIX):
                col = r * d_f + c0 + l * lw
                uf_ref[:, col:col + lw] = (
                    uf_scr[l, pl.ds(r, tm // RADIX, stride=RADIX), :].astype(BF16))
    def put_heads(ref, c0, val):
        dh = ref.shape[-1]
        for c in range(0, val.shape[1], dh):
            ref[(c0 + c) // dh] = val[:, c:c + dh]

    o = d_f
    for c0 in range(0, d_qk, cw):
        put_heads(q_ref, c0, (proj(o + c0, cw) * qscale).astype(BF16))
    o += d_qk
    for c0 in range(0, d_qk, cw):
        need(o + c0, cw)
        k_ref[c0:c0 + cw, :] = lax.dot_general(w_ref[o + c0:o + c0 + cw, :], h, nt,
                                               preferred_element_type=F32).astype(BF16)
    o += d_qk
    for c0 in range(0, d_m, cw):
        put_heads(v_ref, c0, proj(o + c0, cw).astype(BF16))
    o += d_m
    for c0 in range(0, d_m, cw):
        put_heads(og_ref, c0, jax.nn.sigmoid(proj(o + c0, cw)).astype(BF16))
    if load_weights:
        gate_copy().wait()
        wg_ref[...] = jnp.zeros_like(wg_ref)
        wg_ref[0:n_gates, :] = gstage_ref[...].astype(BF16)
    g_ref[...] = lax.dot_general(wg_ref[...], h, nt, preferred_element_type=F32) + bg_ref[...]


def _row_slabs(w, n_steps, first_row=0, n_rows=None, step=lambda *ids: ids[0]):
    n_rows = w.shape[0] - first_row if n_rows is None else n_rows
    rows = n_rows // n_steps
    assert rows * n_steps == n_rows and rows % 16 == 0 and first_row % rows == 0
    return pl.BlockSpec((rows, w.shape[1]), lambda *ids: (step(*ids) + first_row // rows, 0))


W_STAGE_ROWS = 128
W_STAGE_SLOTS = 4


def _inproj(x2, mod, norm1_w, w_in_t, b_gate, w_out, w_up, S, dims):
    T, D = x2.shape
    d_f, d_qk, d_m = dims
    tm = min(512, S)
    tiles_per_seq = S // tm
    n_main = d_f + 2 * d_qk + 2 * d_m
    n_gates = w_in_t.shape[0] - n_main
    sr = min(W_STAGE_ROWS, n_main)
    assert n_main % sr == 0 and n_gates % 16 == 0 and n_gates <= N_GATE_LANES
    n_steps = T // tm
    tok = lambda width: pl.BlockSpec((tm, width), lambda i: (i, 0))
    tok_t = lambda rows: pl.BlockSpec((None, rows, tm),
                                      lambda i: (i // tiles_per_seq, 0, i % tiles_per_seq))
    H = N_HEADS
    tok_h = lambda dh: pl.BlockSpec((None, H, tm, dh),
                                    lambda i: (i // tiles_per_seq, 0, i % tiles_per_seq, 0))
    kern = functools.partial(_inproj_kernel, dims=dims, n_gates=n_gates,
                             qscale=float((d_qk // N_HEADS) ** -0.5))
    return pl.pallas_call(
        kern,
        grid=(T // tm,),
        in_specs=[tok(D),
                  pl.BlockSpec((None, N_MOD, D), lambda i: (i // tiles_per_seq, 0, 0)),
                  pl.BlockSpec((1, D), lambda i: (0, 0)),
                  pl.BlockSpec(memory_space=pl.ANY),
                  pl.BlockSpec((N_GATE_LANES, 1), lambda i: (0, 0)),
                  _row_slabs(w_out, n_steps), _row_slabs(w_up, n_steps, 0, w_up.shape[0] // 2)],
        out_specs=[pl.BlockSpec((tm // RADIX, RADIX * d_f), lambda i: (i, 0)),
                   tok_h(d_qk // H), tok_t(d_qk), tok_h(d_m // H), tok_h(d_m // H),
                   tok_t(N_GATE_LANES),
                   _row_slabs(w_out, n_steps), _row_slabs(w_up, n_steps, 0, w_up.shape[0] // 2)],
        out_shape=[jax.ShapeDtypeStruct((T // RADIX, RADIX * d_f), BF16),
                   jax.ShapeDtypeStruct((T // S, H, S, d_qk // H), BF16),
                   jax.ShapeDtypeStruct((T // S, d_qk, S), BF16),
                   jax.ShapeDtypeStruct((T // S, H, S, d_m // H), BF16),
                   jax.ShapeDtypeStruct((T // S, H, S, d_m // H), BF16),
                   jax.ShapeDtypeStruct((T // S, N_GATE_LANES, S), F32),
                   jax.ShapeDtypeStruct(w_out.shape, BF16),
                   jax.ShapeDtypeStruct((w_up.shape[0] // 2, w_up.shape[1]), BF16)],
        scratch_shapes=[pltpu.VMEM((max(1, min(512, d_qk) // LANES), tm, min(LANES, d_qk)), F32),
                        pltpu.VMEM((n_main, D), BF16),
                        pltpu.VMEM((N_GATE_LANES, D), BF16),
                        pltpu.VMEM((W_STAGE_SLOTS, sr, D), F32),
                        pltpu.VMEM((n_gates, D), F32),
                        pltpu.SemaphoreType.DMA((W_STAGE_SLOTS + 1,))],
        compiler_params=_params(("arbitrary",)),
        name="inproj",
    )(x2, mod, norm1_w.reshape(1, D), w_in_t, b_gate, w_out, w_up)


def _dft_constants(S, gc, tk):
    n1 = S // RADIX
    idx = np.arange(n1, dtype=np.int64)
    ang = 2.0 * np.pi * ((idx[:, None] * idx[None, :]) % n1).astype(np.float64) / n1
    c, s = np.cos(ang), np.sin(ang)
    nt = n1 // tk
    cs = np.concatenate([c.reshape(nt, tk, n1), s.reshape(nt, tk, n1)], axis=1)
    tw = []
    for r in range(1, RADIX):
        a = 2.0 * np.pi * ((r * idx) % S).astype(np.float64) / S
        tw += [np.cos(a), np.sin(a)]
    tw = np.broadcast_to(np.stack(tw)[:, :, None], (2 * (RADIX - 1), n1, LANES))
    ch = np.arange(gc, dtype=np.int64)
    angc = 2.0 * np.pi * ((ch[:, None] * ch[None, :]) % gc).astype(np.float64) / gc
    wch = np.concatenate([np.cos(angc), np.sin(angc)], axis=0) / np.sqrt(float(S) * gc)
    return (jnp.asarray(cs, dtype=BF16), jnp.asarray(np.ascontiguousarray(tw), dtype=F32),
            jnp.asarray(wch, dtype=BF16))


def _fourier_kernel(x_ref, cs_ref, tw_ref, wch_ref, wu32_ref, o_ref, wu16_ref, *, gc, n_groups, tk):
    n1 = x_ref.shape[0]
    _cast_up_slab(wu32_ref, wu16_ref)
    d_f = gc * n_groups
    cs = cs_ref[...]
    wch = wch_ref[...]
    reps = gc // LANES if gc >= LANES else 1

    def lanes(t):
        if gc < LANES:
            return t[:, :gc]
        return jnp.concatenate([t] * reps, axis=1) if reps > 1 else t

    c1, s1, c2, s2, c3, s3 = [lanes(tw_ref[i]) for i in range(2 * (RADIX - 1))]

    def sub_dfts(g):
        return [jnp.dot(cs, x_ref[:, r * d_f + g * gc:r * d_f + (g + 1) * gc],
                        preferred_element_type=F32) for r in range(RADIX)]

    z_next = sub_dfts(0)
    for g in range(n_groups):
        z = z_next
        if g + 1 < n_groups:
            z_next = sub_dfts(g + 1)
        zc, zs = [t[:tk] for t in z], [t[tk:] for t in z]
        t0r, t0i = zc[0], -zs[0]
        t1r, t1i = c1 * zc[1] - s1 * zs[1], -(c1 * zs[1] + s1 * zc[1])
        t2r, t2i = c2 * zc[2] - s2 * zs[2], -(c2 * zs[2] + s2 * zc[2])
        t3r, t3i = c3 * zc[3] - s3 * zs[3], -(c3 * zs[3] + s3 * zc[3])
        ar, ai = t0r + t2r, t0i + t2i
        br, bi = t0r - t2r, t0i - t2i
        cr, ci = t1r + t3r, t1i + t3i
        dr, di = t1r - t3r, t1i - t3i
        xq = [(ar + cr, ai + ci), (br + di, bi - dr), (ar - cr, ai - ci), (br - di, bi + dr)]
        for q in range(RADIX):
            xre, xim = xq[q]
            lhs = jnp.concatenate([xre.astype(BF16), xim.astype(BF16)], axis=1)
            y = jnp.dot(lhs, wch, preferred_element_type=F32)
            o_ref[q, :, g * gc:(g + 1) * gc] = y.astype(BF16)


def _fourier(uf4, w_up, B, S):
    d_f = uf4.shape[-1] // RADIX
    gc = d_f // N_FOURIER_GROUPS
    n1 = S // RADIX
    tk = min(256, n1)
    nt = n1 // tk
    half = w_up.shape[0] // 2
    slab = lambda first: _row_slabs(w_up, B * nt, first, half, step=lambda b, t: b * nt + t)
    cs, tw, wch = _dft_constants(S, gc, tk)
    x4 = uf4.reshape(B, n1, RADIX * d_f)
    kern = functools.partial(_fourier_kernel, gc=gc, n_groups=N_FOURIER_GROUPS, tk=tk)
    out, w_up16_bot = pl.pallas_call(
        kern,
        grid=(B, nt),
        in_specs=[pl.BlockSpec((None, n1, RADIX * d_f), lambda b, t: (b, 0, 0)),
                  pl.BlockSpec((None, 2 * tk, n1), lambda b, t: (t, 0, 0)),
                  pl.BlockSpec((2 * (RADIX - 1), tk, LANES), lambda b, t: (0, t, 0)),
                  pl.BlockSpec((2 * gc, gc), lambda b, t: (0, 0)),
                  slab(half)],
        out_specs=[pl.BlockSpec((None, RADIX, tk, d_f), lambda b, t: (b, 0, t, 0)), slab(0)],
        out_shape=[jax.ShapeDtypeStruct((B, RADIX, n1, d_f), BF16),
                   jax.ShapeDtypeStruct((half, w_up.shape[1]), BF16)],
        compiler_params=_params(("arbitrary", "arbitrary")),
        name="fourier",
    )(x4, cs, tw, wch, w_up)
    return out.reshape(B * S, d_f), w_up16_bot


_ROW_R, _ROW_WK, _ROW_SOLD = 0, 2, 4
_ROW_COLS = 8
_COL_A, _COL_SINT, _COL_EML = 0, 1, 2
N_ROWS = 16


def _lane_scan(x, op, identity, reverse):
    n = x.shape[-1]
    lane = lax.broadcasted_iota(jnp.int32, x.shape, x.ndim - 1)
    shift = 1
    while shift < n:
        if reverse:
            rolled = pltpu.roll(x, n - shift, axis=x.ndim - 1)
            x = op(x, jnp.where(lane < n - shift, rolled, identity))
        else:
            rolled = pltpu.roll(x, shift, axis=x.ndim - 1)
            x = op(x, jnp.where(lane >= shift, rolled, identity))
        shift *= 2
    return x


def _gateprep_kernel(g_ref, o_ref, mst_ref, mnx_ref):
    B, H = o_ref.shape[0], o_ref.shape[1]
    nc, L = g_ref.shape[2], g_ref.shape[3]
    for d in range(2):
        reverse = d == 1
        rows = lambda first: jnp.concatenate(
            [g_ref[b, (2 * d + first) * H:(2 * d + first + 1) * H] for b in range(B)], axis=0)
        ig = rows(0)
        lf = jax.nn.log_sigmoid(rows(1))
        b_cum = _lane_scan(lf, jnp.add, 0.0, reverse)
        g = b_cum[:, :, 0:1] if reverse else b_cum[:, :, L - 1:L]
        r = ig - b_cum
        cm = _lane_scan(r, jnp.maximum, -jnp.inf, reverse)
        m_loc = g + jnp.max(r, axis=-1, keepdims=True)
        m = jnp.full((B * H, 1, 1), M_INIT, F32)
        for step in range(nc):
            c = nc - 1 - step if reverse else step
            mst_ref[:, c:c + 1, :] = m
            m = jnp.maximum(g[:, c:c + 1, :] + m, m_loc[:, c:c + 1, :])
            mnx_ref[:, c:c + 1, :] = m
        m_start, m_next = mst_ref[...], mnx_ref[...]
        top = jnp.maximum(cm, m_start)
        vals = {_ROW_R + d: r,
                _ROW_WK + d: jnp.exp(g + r - m_next),
                _ROW_SOLD + d: jnp.broadcast_to(jnp.exp(g + m_start - m_next), r.shape),
                _ROW_COLS + 4 * d + _COL_A: -top,
                _ROW_COLS + 4 * d + _COL_SINT: jnp.exp(m_start - top),
                _ROW_COLS + 4 * d + _COL_EML: jnp.exp(-(b_cum + top))}
        for row, val in vals.items():
            for b in range(B):
                for h in range(H):
                    o_ref[b, h, row] = val[b * H + h]
    zero = jnp.zeros((nc, L), F32)
    for row in (6, 7, _ROW_COLS + 3, _ROW_COLS + 7):
        for b in range(B):
            for h in range(H):
                o_ref[b, h, row] = zero


def _gateprep(gates_t, B, S):
    H, L = N_HEADS, min(MLSTM_CHUNK, S)
    nc = S // L
    rows = pl.pallas_call(
        _gateprep_kernel,
        out_shape=jax.ShapeDtypeStruct((B, H, N_ROWS, nc, L), F32),
        scratch_shapes=[pltpu.VMEM((B * H, nc, 1), F32), pltpu.VMEM((B * H, nc, 1), F32)],
        name="gateprep",
    )(gates_t[:, :4 * H].reshape(B, 4 * H, nc, L))
    rows = rows.reshape(B, H, N_ROWS, S)
    cols = jnp.swapaxes(rows[:, :, _ROW_COLS:, :], 2, 3)
    return rows, cols


DEN_LANES = LANES


def _lane_tile(x, width):
    n = x.shape[1]
    if width % n == 0:
        return jnp.concatenate([x] * (width // n), axis=1) if width > n else x
    return jnp.broadcast_to(x[:, 0:1], (x.shape[0], width))


def _mlstm_kernel(q_ref, kt_ref, v_ref, og_ref, cols_ref, rows_ref, nw_ref, wd32_ref,
                  o_ref, wd16_ref, h_ref, c_ref, *, L):
    S, dk = q_ref.shape
    dv = v_ref.shape[1]
    nc = S // L
    jj = lax.broadcasted_iota(jnp.int32, (L, L), 0)
    tt = lax.broadcasted_iota(jnp.int32, (L, L), 1)
    ones = jnp.ones((L, DEN_LANES), BF16)
    c_ref[...] = jnp.zeros_like(c_ref)
    wd16_ref[...] = wd32_ref[...].astype(BF16)

    def step(i, _):
        dirs = (0, 1)
        sls = [pl.ds(pl.multiple_of(c * L, L), L) for c in (i, nc - 1 - i)]
        row = lambda d, k: rows_ref[k + d:k + d + 1, sls[d]]
        col = lambda d, k: cols_ref[sls[d], 4 * d + k:4 * d + k + 1]
        qc = [q_ref[sls[d], :] for d in dirs]
        ktc = [kt_ref[:, sls[d]] for d in dirs]
        vaug = [jnp.concatenate([v_ref[sls[d], :], ones], axis=1) for d in dirs]
        state = [c_ref[d] for d in dirs]
        s = [jnp.dot(qc[d], ktc[d], preferred_element_type=F32) for d in dirs]
        kwt = [(ktc[d].astype(F32) * row(d, _ROW_WK)).astype(BF16) for d in dirs]
        upd = [jnp.dot(kwt[d], vaug[d], preferred_element_type=F32) for d in dirs]
        for d in dirs:
            decay = _lane_tile(jnp.broadcast_to(row(d, _ROW_SOLD), (dk, L)), dv + DEN_LANES)
            c_ref[d] = decay * state[d] + upd[d]
        tot = []
        for d in dirs:
            allowed = (tt <= jj) if d == 0 else (tt >= jj)
            p = jnp.where(allowed, jnp.exp(col(d, _COL_A) + row(d, _ROW_R)), 0.0) * s[d]
            lhs = jnp.concatenate(
                [p.astype(BF16), (qc[d].astype(F32) * col(d, _COL_SINT)).astype(BF16)], axis=1)
            rhs = jnp.concatenate([vaug[d], state[d].astype(BF16)], axis=0)
            tot.append(jnp.dot(lhs, rhs, preferred_element_type=F32))
        for d in dirs:
            inv = 1.0 / jnp.maximum(jnp.abs(tot[d][:, dv:]), col(d, _COL_EML))
            h_ref[d, sls[d], :] = tot[d][:, :dv] * _lane_tile(inv, dv)
        return 0

    lax.fori_loop(0, nc, step, 0, unroll=8)

    fb = min(S, 4 * L)

    def finish(i, _):
        sl = pl.ds(pl.multiple_of(i * fb, fb), fb)
        h = h_ref[0, sl, :] + h_ref[1, sl, :]
        y = h * lax.rsqrt(jnp.mean(h * h, axis=-1, keepdims=True) + EPS)
        o_ref[sl, :] = (y * nw_ref[...] * og_ref[sl, :].astype(F32)).astype(BF16)
        return 0

    lax.fori_loop(0, S // fb, finish, 0)


def _mlstm(q, kt, v, og, rows, cols, norm_w, w_down, B, S):
    H = N_HEADS
    dk = q.shape[-1]
    dv = v.shape[-1]
    L = min(MLSTM_CHUNK, S)
    seq = lambda width: pl.BlockSpec((None, None, S, width), lambda b, h: (b, h, 0, 0))
    slab = _row_slabs(w_down, B * H, step=lambda b, h: b * H + h)
    kern = functools.partial(_mlstm_kernel, L=L)
    return pl.pallas_call(
        kern,
        grid=(B, H),
        in_specs=[seq(dk),
                  pl.BlockSpec((None, dk, S), lambda b, h: (b, h, 0)),
                  seq(dv), seq(dv),
                  pl.BlockSpec((None, None, S, cols.shape[-1]), lambda b, h: (b, h, 0, 0)),
                  pl.BlockSpec((None, None, N_ROWS, S), lambda b, h: (b, h, 0, 0)),
                  pl.BlockSpec((1, dv), lambda b, h: (0, h)),
                  slab],
        out_specs=[seq(dv), slab],
        out_shape=[jax.ShapeDtypeStruct((B, H, S, dv), BF16),
                   jax.ShapeDtypeStruct(w_down.shape, BF16)],
        scratch_shapes=[pltpu.VMEM((2, S, dv), F32),
                        pltpu.VMEM((2, dk, dv + DEN_LANES), F32)],
        compiler_params=_params(("arbitrary", "arbitrary")),
        name="mlstm",
    )(q, kt, v, og, cols, rows, norm_w.reshape(1, H * dv), w_down)


def _outproj_kernel(yf_ref, ym_ref, x_ref, mod_ref, nw_ref, w_ref, x1_ref, h2_ref):
    d_f = yf_ref.shape[1]
    tm = x_ref.shape[0]
    n_heads, _, dv = ym_ref.shape
    rc = min(256, tm)
    for r0 in range(0, tm, rc):
        rs = slice(r0, r0 + rc)
        proj = jnp.dot(yf_ref[rs, :], w_ref[0:d_f, :], preferred_element_type=F32)
        for h in range(n_heads):
            proj = proj + jnp.dot(ym_ref[h, rs, :], w_ref[d_f + h * dv:d_f + (h + 1) * dv, :],
                                  preferred_element_type=F32)
        x1 = x_ref[rs, :] + mod_ref[2:3, :] * proj
        x1_ref[rs, :] = x1
        h2_ref[rs, :] = _rms_mod(x1, nw_ref[...], mod_ref[3:4, :], mod_ref[4:5, :]).astype(BF16)


def _outproj(yf, ym, x2, mod, norm2_w, w_out, S):
    T, D = x2.shape
    d_f = yf.shape[1]
    _, H, _, dv = ym.shape
    tm = min(512, S)
    tiles_per_seq = S // tm
    n_steps = T // tm
    tok = lambda width: pl.BlockSpec((tm, width), lambda i: (i, 0))
    return pl.pallas_call(
        _outproj_kernel,
        grid=(n_steps,),
        in_specs=[tok(d_f),
                  pl.BlockSpec((None, H, tm, dv),
                               lambda i: (i // tiles_per_seq, 0, i % tiles_per_seq, 0)),
                  tok(D),
                  pl.BlockSpec((None, N_MOD, D), lambda i: (i // tiles_per_seq, 0, 0)),
                  pl.BlockSpec((1, D), lambda i: (0, 0)),
                  _resident((d_f + H * dv, D), lambda i: (0, 0))],
        out_specs=[tok(D), tok(D)],
        out_shape=[jax.ShapeDtypeStruct((T, D), F32), jax.ShapeDtypeStruct((T, D), BF16)],
        compiler_params=_params(("arbitrary",)),
        name="outproj",
    )(yf, ym, x2, mod, norm2_w.reshape(1, D), w_out)


FFN_COLS = 256
FFN_SUB = 2
CONV_PHASES = 4


def _interleave_gate_value(w, d_ff):
    lead = w.shape[:-1]
    nb = d_ff // FFN_COLS
    w = w.reshape(lead + (2, nb, FFN_COLS))
    return jnp.swapaxes(w, -3, -2).reshape(lead + (2 * d_ff,))


def _ffn_kernel(hp_ref, hc_ref, hn_ref, x1_ref, mod_ref, nfw_ref, wut_ref, wub_ref, cw_ref, cb_ref,
                wd_ref, o_ref, acc_ref, u_ref, y_ref, hext_ref, *, tiles_per_seq):
    i, j = pl.program_id(0), pl.program_id(1)
    tm = hc_ref.shape[0]

    @pl.when(j == 0)
    def _():
        acc_ref[...] = jnp.zeros_like(acc_ref)
        s_idx = i % tiles_per_seq
        before = jnp.where(s_idx > 0, hp_ref[...].astype(F32)[SUBLANES:, :], 0.0)
        after = jnp.where(s_idx < tiles_per_seq - 1, hn_ref[...].astype(F32)[:SUBLANES, :], 0.0)
        hext_ref[0:tm, :] = hc_ref[...]
        hext_ref[tm:, :] = jnp.concatenate([before, after], axis=0).astype(BF16)

    hext = hext_ref[...]

    w2 = 2 * FFN_COLS
    lw = u_ref.shape[-1]
    n_slab = w2 // lw
    rq = tm // CONV_PHASES
    kh = wut_ref.shape[0]
    for s in range(FFN_SUB):
        u = (jnp.dot(hext[:, :kh], wut_ref[:, s * w2:(s + 1) * w2], preferred_element_type=F32)
             + jnp.dot(hext[:, kh:], wub_ref[:, s * w2:(s + 1) * w2], preferred_element_type=F32))
        for l in range(n_slab):
            cols = slice(l * lw, (l + 1) * lw)
            u_ref[s, l, HALO - SUBLANES:HALO, :] = u[tm:tm + SUBLANES, cols]
            u_ref[s, l, HALO:HALO + tm, :] = u[0:tm, cols]
            u_ref[s, l, HALO + tm:HALO + tm + SUBLANES, :] = u[tm + SUBLANES:, cols]
    for s in range(FFN_SUB):
        ys = []
        for l in range(n_slab):
            c0 = s * w2 + l * lw
            cw, cb = cw_ref[:, c0:c0 + lw], cb_ref[:, c0:c0 + lw]
            taps = [u_ref[s, l, pl.ds(HALO - 1 + m, rq, stride=CONV_PHASES), :]
                    for m in range(CONV_PHASES + 2)]
            ys.append(jnp.concatenate(
                [cw[0:1] * taps[k] + cw[1:2] * taps[k + 1] + cw[2:3] * taps[k + 2] + cb
                 for k in range(CONV_PHASES)], axis=0))
        y = jnp.concatenate(ys, axis=1)
        a, v = y[:, :FFN_COLS], y[:, FFN_COLS:]
        gact = (a * jax.nn.sigmoid(a) * v).astype(BF16)
        acc_ref[...] += jnp.dot(gact, wd_ref[s * FFN_COLS:(s + 1) * FFN_COLS, :],
                                preferred_element_type=F32)

    @pl.when(j == pl.num_programs(1) - 1)
    def _():
        D = o_ref.shape[1]
        sq = jnp.zeros((tm, lw), F32)
        for l in range(D // lw):
            cs = slice(l * lw, (l + 1) * lw)
            for k in range(CONV_PHASES):
                y_ref[l, pl.ds(k, rq, stride=CONV_PHASES), :] = acc_ref[k * rq:(k + 1) * rq, cs]
            y = x1_ref[:, cs] + mod_ref[5:6, cs] * y_ref[l]
            y_ref[l] = y
            sq = sq + y * y
        inv = lax.rsqrt(jnp.sum(sq, axis=-1, keepdims=True) / D + EPS)
        for l in range(D // lw):
            cs = slice(l * lw, (l + 1) * lw)
            o_ref[:, cs] = y_ref[l] * inv * nfw_ref[:, cs]


def _ffn(h2, x1, mod, norm_f_w, w_up_top, w_up_bot, w_conv, b_conv, w_down, S):
    T, D = x1.shape
    d_ff = w_down.shape[0]
    tm = min(512, S)
    tn = FFN_SUB * FFN_COLS
    tiles_per_seq = S // tm
    hb = tm // HALO
    last = T // HALO - 1
    kern = functools.partial(_ffn_kernel, tiles_per_seq=tiles_per_seq)
    return pl.pallas_call(
        kern,
        grid=(T // tm, d_ff // tn),
        in_specs=[pl.BlockSpec((HALO, D), lambda i, j: (jnp.maximum(i * hb - 1, 0), 0)),
                  pl.BlockSpec((tm, D), lambda i, j: (i, 0)),
                  pl.BlockSpec((HALO, D), lambda i, j: (jnp.minimum((i + 1) * hb, last), 0)),
                  pl.BlockSpec((tm, D), lambda i, j: (i, 0)),
                  pl.BlockSpec((None, N_MOD, D), lambda i, j: (i // tiles_per_seq, 0, 0)),
                  pl.BlockSpec((1, D), lambda i, j: (0, 0)),
                  pl.BlockSpec((w_up_top.shape[0], 2 * tn), lambda i, j: (0, j)),
                  pl.BlockSpec((w_up_bot.shape[0], 2 * tn), lambda i, j: (0, j)),
                  pl.BlockSpec((3, 2 * tn), lambda i, j: (0, j)),
                  pl.BlockSpec((1, 2 * tn), lambda i, j: (0, j)),
                  pl.BlockSpec((tn, D), lambda i, j: (j, 0))],
        out_specs=pl.BlockSpec((tm, D), lambda i, j: (i, 0)),
        out_shape=jax.ShapeDtypeStruct((T, D), F32),
        scratch_shapes=[pltpu.VMEM((tm, D), F32),
                        pltpu.VMEM((FFN_SUB, 2 * FFN_COLS // LANES, tm + 2 * HALO, LANES), F32),
                        pltpu.VMEM((D // LANES, tm, LANES), F32),
                        pltpu.VMEM((tm + HALO, D), BF16)],
        compiler_params=_params(("arbitrary", "arbitrary")),
        name="ffn",
    )(h2, h2, h2, x1, mod, norm_f_w.reshape(1, D), w_up_top, w_up_bot, w_conv, b_conv, w_down)


def kernel(x, c, w_ada, b_ada, norm1_w, w_in, b_gates, mlstm_norm_w, w_out, norm2_w, w_up,
           w_conv, b_conv, w_down, norm_f_w):
    B, S, D = x.shape
    d_f = D // 2
    d_m = D - d_f
    d_qk = d_m // 2
    dims = (d_f, d_qk, d_m)
    n_main = d_f + 2 * d_qk + 2 * d_m
    n_gates = 4 * N_HEADS
    assert w_in.shape[1] == n_main + n_gates and S % RADIX == 0

    x2 = x.reshape(B * S, D)
    b_gate = jnp.pad(b_gates, (0, N_GATE_LANES - n_gates)).reshape(N_GATE_LANES, 1)

    mod = _adaln(c, w_ada, b_ada)
    uf, q, kt, v, og, gates_t, w_out16, w_up16_top = _inproj(
        x2, mod, norm1_w, w_in.T, b_gate, w_out, w_up, S, dims)
    yf, w_up16_bot = _fourier(uf, w_up, B, S)
    rows, cols = _gateprep(gates_t, B, S)
    ym, w_down16 = _mlstm(q, kt, v, og, rows, cols, mlstm_norm_w, w_down, B, S)
    x1, h2 = _outproj(yf, ym, x2, mod, norm2_w, w_out16, S)
    d_ff = w_down.shape[0]
    out = _ffn(h2, x1, mod, norm_f_w, w_up16_top, w_up16_bot, _interleave_gate_value(w_conv, d_ff),
               _interleave_gate_value(b_conv, d_ff).reshape(1, -1), w_down16, S)
    return out.reshape(B, S, D)
```

```python
import functools

import numpy as np
import jax
import jax.numpy as jnp
from jax import lax
from jax.experimental import pallas as pl
from jax.experimental.pallas import tpu as pltpu

F32 = jnp.float32
BF16 = jnp.bfloat16

N_MOD = 6
N_FOURIER_GROUPS = 4
N_HEADS = 4
RADIX = 4
EPS = 1e-6
M_INIT = -1e30
MLSTM_CHUNK = 128
LANES = 128
SUBLANES = 8
N_GATE_LANES = LANES
HALO = 2 * SUBLANES
V7X_VMEM_LIMIT = 56 * 1024 * 1024


def _params(sem, vmem=V7X_VMEM_LIMIT, flags=None):
    return pltpu.CompilerParams(dimension_semantics=sem, vmem_limit_bytes=vmem, flags=flags)


def _resident(block_shape, index_map):
    return pl.BlockSpec(block_shape, index_map, pipeline_mode=pl.Buffered(1))


def _adaln_kernel(c_ref, w_ref, b_ref, o_ref):
    c = c_ref[...]
    s = c * jax.nn.sigmoid(c)
    s_hi = s.astype(BF16).astype(F32)
    row = lax.broadcasted_iota(jnp.int32, s.shape, 0)
    lhs = jnp.where(row < SUBLANES, s_hi, s - s_hi).astype(BF16)
    w = w_ref[...]
    w_hi = w.astype(BF16)
    w_lo = (w - w_hi.astype(F32)).astype(BF16)
    r = jnp.dot(lhs, w_hi, preferred_element_type=F32)
    r = r + jnp.dot(lhs, w_lo, preferred_element_type=F32)
    o_ref[...] = r[0:SUBLANES] + r[SUBLANES:2 * SUBLANES] + b_ref[...]


def _adaln(c, w_ada, b_ada):
    B, D = c.shape
    N = w_ada.shape[1]
    assert B <= SUBLANES
    tn = min(2048, D)
    cp = jnp.zeros((2 * SUBLANES, D), F32).at[0:B].set(c).at[SUBLANES:SUBLANES + B].set(c)
    out = pl.pallas_call(
        _adaln_kernel,
        grid=(N // tn,),
        in_specs=[pl.BlockSpec((2 * SUBLANES, D), lambda j: (0, 0)),
                  pl.BlockSpec((D, tn), lambda j: (0, j)),
                  pl.BlockSpec((1, tn), lambda j: (0, j))],
        out_specs=pl.BlockSpec((SUBLANES, tn), lambda j: (0, j)),
        out_shape=jax.ShapeDtypeStruct((SUBLANES, N), F32),
        compiler_params=_params(("arbitrary",)),
        name="adaln",
    )(cp, w_ada, b_ada.reshape(1, N))
    return out[:B].reshape(B, N_MOD, D)


def _rms_mod(x, nw, shift, scale):
    ms = jnp.mean(x * x, axis=-1, keepdims=True)
    return x * lax.rsqrt(ms + EPS) * (nw * (1.0 + scale)) + shift


def _cast_up_slab(src_ref, dst_ref):
    d_ff = src_ref.shape[1] // 2
    for g in range(d_ff // FFN_COLS):
        for half in range(2):
            src = half * d_ff + g * FFN_COLS
            dst = (2 * g + half) * FFN_COLS
            dst_ref[:, dst:dst + FFN_COLS] = src_ref[:, src:src + FFN_COLS].astype(BF16)


def _inproj_kernel(*refs, **static):
    first = pl.program_id(0) == 0
    pl.when(first)(functools.partial(_inproj_body, *refs, load_weights=True, **static))
    pl.when(jnp.logical_not(first))(functools.partial(_inproj_body, *refs, load_weights=False, **static))


def _inproj_body(x_ref, mod_ref, nw_ref, win_hbm, bg_ref, wo32_ref, wu32_ref,
                 uf_ref, q_ref, k_ref, v_ref, og_ref, g_ref, wo16_ref, wu16_ref,
                 uf_scr, w_ref, wg_ref, stage_ref, gstage_ref, sem,
                 *, dims, qscale, n_gates, load_weights):
    d_f, d_qk, d_m = dims
    n_main = w_ref.shape[0]
    tm = x_ref.shape[0]
    n_slots, sr = stage_ref.shape[0], stage_ref.shape[1]
    n_chunks = n_main // sr
    ahead = n_slots - 1

    def chunk_copy(k):
        slot = k % n_slots
        return pltpu.make_async_copy(win_hbm.at[pl.ds(k * sr, sr), :], stage_ref.at[slot], sem.at[slot])

    def gate_copy():
        return pltpu.make_async_copy(win_hbm.at[pl.ds(n_main, n_gates), :], gstage_ref, sem.at[n_slots])

    n_loaded = [0]

    def need(c0, width):
        if not load_weights:
            return
        upto = -(-(c0 + width) // sr)
        for k in range(n_loaded[0], upto):
            chunk_copy(k).wait()
            if k + ahead < n_chunks:
                chunk_copy(k + ahead).start()
            w_ref[k * sr:(k + 1) * sr, :] = stage_ref[k % n_slots].astype(BF16)
        n_loaded[0] = max(n_loaded[0], upto)

    if load_weights:
        gate_copy().start()
        for k in range(min(ahead, n_chunks)):
            chunk_copy(k).start()
    wo16_ref[...] = wo32_ref[...].astype(BF16)
    _cast_up_slab(wu32_ref, wu16_ref)
    h = _rms_mod(x_ref[...], nw_ref[...], mod_ref[0:1, :], mod_ref[1:2, :]).astype(BF16)

    nt = (((1,), (1,)), ((), ()))

    def proj(c0, width):
        need(c0, width)
        return lax.dot_general(h, w_ref[c0:c0 + width, :], nt, preferred_element_type=F32)

    cw = min(512, d_qk)
    lw = uf_scr.shape[-1]
    for c0 in range(0, d_f, cw):
        u = proj(c0, cw)
        for l in range(cw // lw):
            uf_scr[l] = u[:, l * lw:(l + 1) * lw]
        for l in range(cw // lw):
            for r in range(RADIX):
                col = r * d_f + c0 + l * lw
                uf_ref[:, col:col + lw] = (
                    uf_scr[l, pl.ds(r, tm // RADIX, stride=RADIX), :].astype(BF16))
    def put_heads(ref, c0, val):
        dh = ref.shape[-1]
        for c in range(0, val.shape[1], dh):
            ref[(c0 + c) // dh] = val[:, c:c + dh]

    o = d_f
    for c0 in range(0, d_qk, cw):
        put_heads(q_ref, c0, (proj(o + c0, cw) * qscale).astype(BF16))
    o += d_qk
    for c0 in range(0, d_qk, cw):
        need(o + c0, cw)
        k_ref[c0:c0 + cw, :] = lax.dot_general(w_ref[o + c0:o + c0 + cw, :], h, nt,
                                               preferred_element_type=F32).astype(BF16)
    o += d_qk
    for c0 in range(0, d_m, cw):
        put_heads(v_ref, c0, proj(o + c0, cw).astype(BF16))
    o += d_m
    for c0 in range(0, d_m, cw):
        put_heads(og_ref, c0, jax.nn.sigmoid(proj(o + c0, cw)).astype(BF16))
    if load_weights:
        gate_copy().wait()
        wg_ref[...] = jnp.zeros_like(wg_ref)
        wg_ref[0:n_gates, :] = gstage_ref[...].astype(BF16)
    g_ref[...] = lax.dot_general(wg_ref[...], h, nt, preferred_element_type=F32) + bg_ref[...]


def _row_slabs(w, n_steps, first_row=0, n_rows=None, step=lambda *ids: ids[0]):
    n_rows = w.shape[0] - first_row if n_rows is None else n_rows
    rows = n_rows // n_steps
    assert rows * n_steps == n_rows and rows % 16 == 0 and first_row % rows == 0
    return pl.BlockSpec((rows, w.shape[1]), lambda *ids: (step(*ids) + first_row // rows, 0))


W_STAGE_ROWS = 128
W_STAGE_SLOTS = 4


def _inproj(x2, mod, norm1_w, w_in_t, b_gate, w_out, w_up, S, dims):
    T, D = x2.shape
    d_f, d_qk, d_m = dims
    tm = min(512, S)
    tiles_per_seq = S // tm
    n_main = d_f + 2 * d_qk + 2 * d_m
    n_gates = w_in_t.shape[0] - n_main
    sr = min(W_STAGE_ROWS, n_main)
    assert n_main % sr == 0 and n_gates % 16 == 0 and n_gates <= N_GATE_LANES
    n_steps = T // tm
    tok = lambda width: pl.BlockSpec((tm, width), lambda i: (i, 0))
    tok_t = lambda rows: pl.BlockSpec((None, rows, tm),
                                      lambda i: (i // tiles_per_seq, 0, i % tiles_per_seq))
    H = N_HEADS
    tok_h = lambda dh: pl.BlockSpec((None, H, tm, dh),
                                    lambda i: (i // tiles_per_seq, 0, i % tiles_per_seq, 0))
    kern = functools.partial(_inproj_kernel, dims=dims, n_gates=n_gates,
                             qscale=float((d_qk // N_HEADS) ** -0.5))
    return pl.pallas_call(
        kern,
        grid=(T // tm,),
        in_specs=[tok(D),
                  pl.BlockSpec((None, N_MOD, D), lambda i: (i // tiles_per_seq, 0, 0)),
                  pl.BlockSpec((1, D), lambda i: (0, 0)),
                  pl.BlockSpec(memory_space=pl.ANY),
                  pl.BlockSpec((N_GATE_LANES, 1), lambda i: (0, 0)),
                  _row_slabs(w_out, n_steps), _row_slabs(w_up, n_steps, 0, w_up.shape[0] // 2)],
        out_specs=[pl.BlockSpec((tm // RADIX, RADIX * d_f), lambda i: (i, 0)),
                   tok_h(d_qk // H), tok_t(d_qk), tok_h(d_m // H), tok_h(d_m // H),
                   tok_t(N_GATE_LANES),
                   _row_slabs(w_out, n_steps), _row_slabs(w_up, n_steps, 0, w_up.shape[0] // 2)],
        out_shape=[jax.ShapeDtypeStruct((T // RADIX, RADIX * d_f), BF16),
                   jax.ShapeDtypeStruct((T // S, H, S, d_qk // H), BF16),
                   jax.ShapeDtypeStruct((T // S, d_qk, S), BF16),
                   jax.ShapeDtypeStruct((T // S, H, S, d_m // H), BF16),
                   jax.ShapeDtypeStruct((T // S, H, S, d_m // H), BF16),
                   jax.ShapeDtypeStruct((T // S, N_GATE_LANES, S), F32),
                   jax.ShapeDtypeStruct(w_out.shape, BF16),
                   jax.ShapeDtypeStruct((w_up.shape[0] // 2, w_up.shape[1]), BF16)],
        scratch_shapes=[pltpu.VMEM((max(1, min(512, d_qk) // LANES), tm, min(LANES, d_qk)), F32),
                        pltpu.VMEM((n_main, D), BF16),
                        pltpu.VMEM((N_GATE_LANES, D), BF16),
                        pltpu.VMEM((W_STAGE_SLOTS, sr, D), F32),
                        pltpu.VMEM((n_gates, D), F32),
                        pltpu.SemaphoreType.DMA((W_STAGE_SLOTS + 1,))],
        compiler_params=_params(("arbitrary",)),
        name="inproj",
    )(x2, mod, norm1_w.reshape(1, D), w_in_t, b_gate, w_out, w_up)


def _dft_constants(S, gc, tk):
    n1 = S // RADIX
    idx = np.arange(n1, dtype=np.int64)
    ang = 2.0 * np.pi * ((idx[:, None] * idx[None, :]) % n1).astype(np.float64) / n1
    c, s = np.cos(ang), np.sin(ang)
    nt = n1 // tk
    cs = np.concatenate([c.reshape(nt, tk, n1), s.reshape(nt, tk, n1)], axis=1)
    tw = []
    for r in range(1, RADIX):
        a = 2.0 * np.pi * ((r * idx) % S).astype(np.float64) / S
        tw += [np.cos(a), np.sin(a)]
    tw = np.broadcast_to(np.stack(tw)[:, :, None], (2 * (RADIX - 1), n1, LANES))
    ch = np.arange(gc, dtype=np.int64)
    angc = 2.0 * np.pi * ((ch[:, None] * ch[None, :]) % gc).astype(np.float64) / gc
    wch = np.concatenate([np.cos(angc), np.sin(angc)], axis=0) / np.sqrt(float(S) * gc)
    return (jnp.asarray(cs, dtype=BF16), jnp.asarray(np.ascontiguousarray(tw), dtype=F32),
            jnp.asarray(wch, dtype=BF16))


def _fourier_kernel(x_ref, cs_ref, tw_ref, wch_ref, wu32_ref, o_ref, wu16_ref, *, gc, n_groups, tk):
    n1 = x_ref.shape[0]
    _cast_up_slab(wu32_ref, wu16_ref)
    d_f = gc * n_groups
    cs = cs_ref[...]
    wch = wch_ref[...]
    reps = gc // LANES if gc >= LANES else 1

    def lanes(t):
        if gc < LANES:
            return t[:, :gc]
        return jnp.concatenate([t] * reps, axis=1) if reps > 1 else t

    c1, s1, c2, s2, c3, s3 = [lanes(tw_ref[i]) for i in range(2 * (RADIX - 1))]

    def sub_dfts(g):
        return [jnp.dot(cs, x_ref[:, r * d_f + g * gc:r * d_f + (g + 1) * gc],
                        preferred_element_type=F32) for r in range(RADIX)]

    z_next = sub_dfts(0)
    for g in range(n_groups):
        z = z_next
        if g + 1 < n_groups:
            z_next = sub_dfts(g + 1)
        zc, zs = [t[:tk] for t in z], [t[tk:] for t in z]
        t0r, t0i = zc[0], -zs[0]
        t1r, t1i = c1 * zc[1] - s1 * zs[1], -(c1 * zs[1] + s1 * zc[1])
        t2r, t2i = c2 * zc[2] - s2 * zs[2], -(c2 * zs[2] + s2 * zc[2])
        t3r, t3i = c3 * zc[3] - s3 * zs[3], -(c3 * zs[3] + s3 * zc[3])
        ar, ai = t0r + t2r, t0i + t2i
        br, bi = t0r - t2r, t0i - t2i
        cr, ci = t1r + t3r, t1i + t3i
        dr, di = t1r - t3r, t1i - t3i
        xq = [(ar + cr, ai + ci), (br + di, bi - dr), (ar - cr, ai - ci), (br - di, bi + dr)]
        for q in range(RADIX):
            xre, xim = xq[q]
            lhs = jnp.concatenate([xre.astype(BF16), xim.astype(BF16)], axis=1)
            y = jnp.dot(lhs, wch, preferred_element_type=F32)
            o_ref[q, :, g * gc:(g + 1) * gc] = y.astype(BF16)


def _fourier(uf4, w_up, B, S):
    d_f = uf4.shape[-1] // RADIX
    gc = d_f // N_FOURIER_GROUPS
    n1 = S // RADIX
    tk = min(256, n1)
    nt = n1 // tk
    half = w_up.shape[0] // 2
    slab = lambda first: _row_slabs(w_up, B * nt, first, half, step=lambda b, t: b * nt + t)
    cs, tw, wch = _dft_constants(S, gc, tk)
    x4 = uf4.reshape(B, n1, RADIX * d_f)
    kern = functools.partial(_fourier_kernel, gc=gc, n_groups=N_FOURIER_GROUPS, tk=tk)
    out, w_up16_bot = pl.pallas_call(
        kern,
        grid=(B, nt),
        in_specs=[pl.BlockSpec((None, n1, RADIX * d_f), lambda b, t: (b, 0, 0)),
                  pl.BlockSpec((None, 2 * tk, n1), lambda b, t: (t, 0, 0)),
                  pl.BlockSpec((2 * (RADIX - 1), tk, LANES), lambda b, t: (0, t, 0)),
                  pl.BlockSpec((2 * gc, gc), lambda b, t: (0, 0)),
                  slab(half)],
        out_specs=[pl.BlockSpec((None, RADIX, tk, d_f), lambda b, t: (b, 0, t, 0)), slab(0)],
        out_shape=[jax.ShapeDtypeStruct((B, RADIX, n1, d_f), BF16),
                   jax.ShapeDtypeStruct((half, w_up.shape[1]), BF16)],
        compiler_params=_params(("arbitrary", "arbitrary")),
        name="fourier",
    )(x4, cs, tw, wch, w_up)
    return out.reshape(B * S, d_f), w_up16_bot


_ROW_R, _ROW_WK, _ROW_SOLD, _ROW_MST = 0, 2, 4, 6
_ROW_COLS = 8
_COL_A, _COL_UNUSED, _COL_EML = 0, 1, 2
N_ROWS = 16


def _lane_scan(x, op, identity, reverse):
    n = x.shape[-1]
    lane = lax.broadcasted_iota(jnp.int32, x.shape, x.ndim - 1)
    shift = 1
    while shift < n:
        if reverse:
            rolled = pltpu.roll(x, n - shift, axis=x.ndim - 1)
            x = op(x, jnp.where(lane < n - shift, rolled, identity))
        else:
            rolled = pltpu.roll(x, shift, axis=x.ndim - 1)
            x = op(x, jnp.where(lane >= shift, rolled, identity))
        shift *= 2
    return x


def _gateprep_kernel(g_ref, o_ref, mst_ref, mnx_ref):
    B, H = o_ref.shape[0], o_ref.shape[1]
    nc, L = g_ref.shape[2], g_ref.shape[3]
    for d in range(2):
        reverse = d == 1
        rows = lambda first: jnp.concatenate(
            [g_ref[b, (2 * d + first) * H:(2 * d + first + 1) * H] for b in range(B)], axis=0)
        ig = rows(0)
        lf = jax.nn.log_sigmoid(rows(1))
        b_cum = _lane_scan(lf, jnp.add, 0.0, reverse)
        g = b_cum[:, :, 0:1] if reverse else b_cum[:, :, L - 1:L]
        r = ig - b_cum
        cm = _lane_scan(r, jnp.maximum, -jnp.inf, reverse)
        m_loc = g + jnp.max(r, axis=-1, keepdims=True)
        m = jnp.full((B * H, 1, 1), M_INIT, F32)
        for step in range(nc):
            c = nc - 1 - step if reverse else step
            mst_ref[:, c:c + 1, :] = m
            m = jnp.maximum(g[:, c:c + 1, :] + m, m_loc[:, c:c + 1, :])
            mnx_ref[:, c:c + 1, :] = m
        m_start, m_next = mst_ref[...], mnx_ref[...]
        top = jnp.maximum(cm, m_start)
        vals = {_ROW_R + d: r,
                _ROW_WK + d: jnp.exp(g + r - m_next),
                _ROW_SOLD + d: jnp.broadcast_to(jnp.exp(g + m_start - m_next), r.shape),
                _ROW_MST + d: jnp.broadcast_to(m_start, r.shape),
                _ROW_COLS + 4 * d + _COL_A: -top,
                _ROW_COLS + 4 * d + _COL_EML: jnp.exp(-(b_cum + top))}
        for row, val in vals.items():
            for b in range(B):
                for h in range(H):
                    o_ref[b, h, row] = val[b * H + h]
    zero = jnp.zeros((nc, L), F32)
    for row in [_ROW_COLS + 4 * d + k for d in range(2) for k in (_COL_UNUSED, 3)]:
        for b in range(B):
            for h in range(H):
                o_ref[b, h, row] = zero


def _gateprep(gates_t, B, S):
    H, L = N_HEADS, min(MLSTM_CHUNK, S)
    nc = S // L
    rows = pl.pallas_call(
        _gateprep_kernel,
        out_shape=jax.ShapeDtypeStruct((B, H, N_ROWS, nc, L), F32),
        scratch_shapes=[pltpu.VMEM((B * H, nc, 1), F32), pltpu.VMEM((B * H, nc, 1), F32)],
        name="gateprep",
    )(gates_t[:, :4 * H].reshape(B, 4 * H, nc, L))
    rows = rows.reshape(B, H, N_ROWS, S)
    cols = jnp.swapaxes(rows[:, :, _ROW_COLS:, :], 2, 3)
    return rows, cols


DEN_LANES = LANES


def _lane_tile(x, width):
    n = x.shape[1]
    if width % n == 0:
        return jnp.concatenate([x] * (width // n), axis=1) if width > n else x
    return jnp.broadcast_to(x[:, 0:1], (x.shape[0], width))


def _mlstm_kernel(q_ref, kt_ref, v_ref, og_ref, cols_ref, rows_ref, nw_ref, wd32_ref,
                  o_ref, wd16_ref, h_ref, c_ref, *, L):
    S, dk = q_ref.shape
    dv = v_ref.shape[1]
    nc = S // L
    jj = lax.broadcasted_iota(jnp.int32, (L, L), 0)
    tt = lax.broadcasted_iota(jnp.int32, (L, L), 1)
    ones = jnp.ones((L, DEN_LANES), BF16)
    c_ref[...] = jnp.zeros_like(c_ref)
    wd16_ref[...] = wd32_ref[...].astype(BF16)

    def step(i, _):
        dirs = (0, 1)
        sls = [pl.ds(pl.multiple_of(c * L, L), L) for c in (i, nc - 1 - i)]
        row = lambda d, k: rows_ref[k + d:k + d + 1, sls[d]]
        col = lambda d, k: cols_ref[sls[d], 4 * d + k:4 * d + k + 1]
        qc = [q_ref[sls[d], :] for d in dirs]
        ktc = [kt_ref[:, sls[d]] for d in dirs]
        vaug = [jnp.concatenate([v_ref[sls[d], :], ones], axis=1) for d in dirs]
        state = [c_ref[d] for d in dirs]
        s = [jnp.dot(qc[d], ktc[d], preferred_element_type=F32) for d in dirs]
        kwt = [(ktc[d].astype(F32) * row(d, _ROW_WK)).astype(BF16) for d in dirs]
        upd = [jnp.dot(kwt[d], vaug[d], preferred_element_type=F32) for d in dirs]
        for d in dirs:
            decay = _lane_tile(jnp.broadcast_to(row(d, _ROW_SOLD), (dk, L)), dv + DEN_LANES)
            c_ref[d] = decay * state[d] + upd[d]
        tot = []
        for d in dirs:
            allowed = (tt <= jj) if d == 0 else (tt >= jj)
            col_a = col(d, _COL_A)
            p = jnp.where(allowed, jnp.exp(col_a + row(d, _ROW_R)), 0.0) * s[d]
            s_inter = jnp.exp(jnp.broadcast_to(col_a, (L, dk)) + row(d, _ROW_MST)[:, :dk])
            lhs = jnp.concatenate(
                [p.astype(BF16), (qc[d].astype(F32) * s_inter).astype(BF16)], axis=1)
            rhs = jnp.concatenate([vaug[d], state[d].astype(BF16)], axis=0)
            tot.append(jnp.dot(lhs, rhs, preferred_element_type=F32))
        for d in dirs:
            inv = 1.0 / jnp.maximum(jnp.abs(tot[d][:, dv:]), col(d, _COL_EML))
            h_ref[d, sls[d], :] = tot[d][:, :dv] * _lane_tile(inv, dv)
        return 0

    lax.fori_loop(0, nc, step, 0, unroll=8)

    fb = min(S, 4 * L)

    def finish(i, _):
        sl = pl.ds(pl.multiple_of(i * fb, fb), fb)
        h = h_ref[0, sl, :] + h_ref[1, sl, :]
        y = h * lax.rsqrt(jnp.mean(h * h, axis=-1, keepdims=True) + EPS)
        o_ref[sl, :] = (y * nw_ref[...] * og_ref[sl, :].astype(F32)).astype(BF16)
        return 0

    lax.fori_loop(0, S // fb, finish, 0)


def _mlstm(q, kt, v, og, rows, cols, norm_w, w_down, B, S):
    H = N_HEADS
    dk = q.shape[-1]
    dv = v.shape[-1]
    L = min(MLSTM_CHUNK, S)
    seq = lambda width: pl.BlockSpec((None, None, S, width), lambda b, h: (b, h, 0, 0))
    slab = _row_slabs(w_down, B * H, step=lambda b, h: b * H + h)
    kern = functools.partial(_mlstm_kernel, L=L)
    return pl.pallas_call(
        kern,
        grid=(B, H),
        in_specs=[seq(dk),
                  pl.BlockSpec((None, dk, S), lambda b, h: (b, h, 0)),
                  seq(dv), seq(dv),
                  pl.BlockSpec((None, None, S, cols.shape[-1]), lambda b, h: (b, h, 0, 0)),
                  pl.BlockSpec((None, None, N_ROWS, S), lambda b, h: (b, h, 0, 0)),
                  pl.BlockSpec((1, dv), lambda b, h: (0, h)),
                  slab],
        out_specs=[seq(dv), slab],
        out_shape=[jax.ShapeDtypeStruct((B, H, S, dv), BF16),
                   jax.ShapeDtypeStruct(w_down.shape, BF16)],
        scratch_shapes=[pltpu.VMEM((2, S, dv), F32),
                        pltpu.VMEM((2, dk, dv + DEN_LANES), F32)],
        compiler_params=_params(("arbitrary", "arbitrary")),
        name="mlstm",
    )(q, kt, v, og, cols, rows, norm_w.reshape(1, H * dv), w_down)


def _outproj_kernel(yf_ref, ym_ref, x_ref, mod_ref, nw_ref, w_ref, x1_ref, h2_ref):
    d_f = yf_ref.shape[1]
    tm = x_ref.shape[0]
    n_heads, _, dv = ym_ref.shape
    rc = min(256, tm)
    for r0 in range(0, tm, rc):
        rs = slice(r0, r0 + rc)
        proj = jnp.dot(yf_ref[rs, :], w_ref[0:d_f, :], preferred_element_type=F32)
        for h in range(n_heads):
            proj = proj + jnp.dot(ym_ref[h, rs, :], w_ref[d_f + h * dv:d_f + (h + 1) * dv, :],
                                  preferred_element_type=F32)
        x1 = x_ref[rs, :] + mod_ref[2:3, :] * proj
        x1_ref[rs, :] = x1
        h2_ref[rs, :] = _rms_mod(x1, nw_ref[...], mod_ref[3:4, :], mod_ref[4:5, :]).astype(BF16)


def _outproj(yf, ym, x2, mod, norm2_w, w_out, S):
    T, D = x2.shape
    d_f = yf.shape[1]
    _, H, _, dv = ym.shape
    tm = min(512, S)
    tiles_per_seq = S // tm
    n_steps = T // tm
    tok = lambda width: pl.BlockSpec((tm, width), lambda i: (i, 0))
    return pl.pallas_call(
        _outproj_kernel,
        grid=(n_steps,),
        in_specs=[tok(d_f),
                  pl.BlockSpec((None, H, tm, dv),
                               lambda i: (i // tiles_per_seq, 0, i % tiles_per_seq, 0)),
                  tok(D),
                  pl.BlockSpec((None, N_MOD, D), lambda i: (i // tiles_per_seq, 0, 0)),
                  pl.BlockSpec((1, D), lambda i: (0, 0)),
                  _resident((d_f + H * dv, D), lambda i: (0, 0))],
        out_specs=[tok(D), tok(D)],
        out_shape=[jax.ShapeDtypeStruct((T, D), F32), jax.ShapeDtypeStruct((T, D), BF16)],
        compiler_params=_params(("arbitrary",)),
        name="outproj",
    )(yf, ym, x2, mod, norm2_w.reshape(1, D), w_out)


FFN_COLS = 256
FFN_SUB = 2
CONV_PHASES = 4


def _interleave_gate_value(w, d_ff):
    lead = w.shape[:-1]
    nb = d_ff // FFN_COLS
    w = w.reshape(lead + (2, nb, FFN_COLS))
    return jnp.swapaxes(w, -3, -2).reshape(lead + (2 * d_ff,))


def _ffn_kernel(hp_ref, hc_ref, hn_ref, x1_ref, mod_ref, nfw_ref, wut_ref, wub_ref, cw_ref, cb_ref,
                wd_ref, o_ref, acc_ref, u_ref, y_ref, hext_ref, *, tiles_per_seq):
    i, j = pl.program_id(0), pl.program_id(1)
    tm = hc_ref.shape[0]

    @pl.when(j == 0)
    def _():
        acc_ref[...] = jnp.zeros_like(acc_ref)
        s_idx = i % tiles_per_seq
        before = jnp.where(s_idx > 0, hp_ref[...].astype(F32)[SUBLANES:, :], 0.0)
        after = jnp.where(s_idx < tiles_per_seq - 1, hn_ref[...].astype(F32)[:SUBLANES, :], 0.0)
        hext_ref[0:tm, :] = hc_ref[...]
        hext_ref[tm:, :] = jnp.concatenate([before, after], axis=0).astype(BF16)

    hext = hext_ref[...]

    w2 = 2 * FFN_COLS
    lw = u_ref.shape[-1]
    n_slab = w2 // lw
    rq = tm // CONV_PHASES
    kh = wut_ref.shape[0]
    for s in range(FFN_SUB):
        u = (jnp.dot(hext[:, :kh], wut_ref[:, s * w2:(s + 1) * w2], preferred_element_type=F32)
             + jnp.dot(hext[:, kh:], wub_ref[:, s * w2:(s + 1) * w2], preferred_element_type=F32))
        for l in range(n_slab):
            cols = slice(l * lw, (l + 1) * lw)
            u_ref[s, l, HALO - SUBLANES:HALO, :] = u[tm:tm + SUBLANES, cols]
            u_ref[s, l, HALO:HALO + tm, :] = u[0:tm, cols]
            u_ref[s, l, HALO + tm:HALO + tm + SUBLANES, :] = u[tm + SUBLANES:, cols]
    for s in range(FFN_SUB):
        ys = []
        for l in range(n_slab):
            c0 = s * w2 + l * lw
            cw, cb = cw_ref[:, c0:c0 + lw], cb_ref[:, c0:c0 + lw]
            taps = [u_ref[s, l, pl.ds(HALO - 1 + m, rq, stride=CONV_PHASES), :]
                    for m in range(CONV_PHASES + 2)]
            ys.append(jnp.concatenate(
                [cw[0:1] * taps[k] + cw[1:2] * taps[k + 1] + cw[2:3] * taps[k + 2] + cb
                 for k in range(CONV_PHASES)], axis=0))
        y = jnp.concatenate(ys, axis=1)
        a, v = y[:, :FFN_COLS], y[:, FFN_COLS:]
        gact = (a * jax.nn.sigmoid(a) * v).astype(BF16)
        acc_ref[...] += jnp.dot(gact, wd_ref[s * FFN_COLS:(s + 1) * FFN_COLS, :],
                                preferred_element_type=F32)

    @pl.when(j == pl.num_programs(1) - 1)
    def _():
        D = o_ref.shape[1]
        sq = jnp.zeros((tm, lw), F32)
        for l in range(D // lw):
            cs = slice(l * lw, (l + 1) * lw)
            for k in range(CONV_PHASES):
                y_ref[l, pl.ds(k, rq, stride=CONV_PHASES), :] = acc_ref[k * rq:(k + 1) * rq, cs]
            y = x1_ref[:, cs] + mod_ref[5:6, cs] * y_ref[l]
            y_ref[l] = y
            sq = sq + y * y
        inv = lax.rsqrt(jnp.sum(sq, axis=-1, keepdims=True) / D + EPS)
        for l in range(D // lw):
            cs = slice(l * lw, (l + 1) * lw)
            o_ref[:, cs] = y_ref[l] * inv * nfw_ref[:, cs]


def _ffn(h2, x1, mod, norm_f_w, w_up_top, w_up_bot, w_conv, b_conv, w_down, S):
    T, D = x1.shape
    d_ff = w_down.shape[0]
    tm = min(512, S)
    tn = FFN_SUB * FFN_COLS
    tiles_per_seq = S // tm
    hb = tm // HALO
    last = T // HALO - 1
    kern = functools.partial(_ffn_kernel, tiles_per_seq=tiles_per_seq)
    return pl.pallas_call(
        kern,
        grid=(T // tm, d_ff // tn),
        in_specs=[pl.BlockSpec((HALO, D), lambda i, j: (jnp.maximum(i * hb - 1, 0), 0)),
                  pl.BlockSpec((tm, D), lambda i, j: (i, 0)),
                  pl.BlockSpec((HALO, D), lambda i, j: (jnp.minimum((i + 1) * hb, last), 0)),
                  pl.BlockSpec((tm, D), lambda i, j: (i, 0)),
                  pl.BlockSpec((None, N_MOD, D), lambda i, j: (i // tiles_per_seq, 0, 0)),
                  pl.BlockSpec((1, D), lambda i, j: (0, 0)),
                  pl.BlockSpec((w_up_top.shape[0], 2 * tn), lambda i, j: (0, j)),
                  pl.BlockSpec((w_up_bot.shape[0], 2 * tn), lambda i, j: (0, j)),
                  pl.BlockSpec((3, 2 * tn), lambda i, j: (0, j)),
                  pl.BlockSpec((1, 2 * tn), lambda i, j: (0, j)),
                  pl.BlockSpec((tn, D), lambda i, j: (j, 0))],
        out_specs=pl.BlockSpec((tm, D), lambda i, j: (i, 0)),
        out_shape=jax.ShapeDtypeStruct((T, D), F32),
        scratch_shapes=[pltpu.VMEM((tm, D), F32),
                        pltpu.VMEM((FFN_SUB, 2 * FFN_COLS // LANES, tm + 2 * HALO, LANES), F32),
                        pltpu.VMEM((D // LANES, tm, LANES), F32),
                        pltpu.VMEM((tm + HALO, D), BF16)],
        compiler_params=_params(("arbitrary", "arbitrary")),
        name="ffn",
    )(h2, h2, h2, x1, mod, norm_f_w.reshape(1, D), w_up_top, w_up_bot, w_conv, b_conv, w_down)


def kernel(x, c, w_ada, b_ada, norm1_w, w_in, b_gates, mlstm_norm_w, w_out, norm2_w, w_up,
           w_conv, b_conv, w_down, norm_f_w):
    B, S, D = x.shape
    d_f = D // 2
    d_m = D - d_f
    d_qk = d_m // 2
    dims = (d_f, d_qk, d_m)
    n_main = d_f + 2 * d_qk + 2 * d_m
    n_gates = 4 * N_HEADS
    assert w_in.shape[1] == n_main + n_gates and S % RADIX == 0

    x2 = x.reshape(B * S, D)
    b_gate = jnp.pad(b_gates, (0, N_GATE_LANES - n_gates)).reshape(N_GATE_LANES, 1)

    mod = _adaln(c, w_ada, b_ada)
    uf, q, kt, v, og, gates_t, w_out16, w_up16_top = _inproj(
        x2, mod, norm1_w, w_in.T, b_gate, w_out, w_up, S, dims)
    yf, w_up16_bot = _fourier(uf, w_up, B, S)
    rows, cols = _gateprep(gates_t, B, S)
    ym, w_down16 = _mlstm(q, kt, v, og, rows, cols, mlstm_norm_w, w_down, B, S)
    x1, h2 = _outproj(yf, ym, x2, mod, norm2_w, w_out16, S)
    d_ff = w_down.shape[0]
    out = _ffn(h2, x1, mod, norm_f_w, w_up16_top, w_up16_bot, _interleave_gate_value(w_conv, d_ff),
               _interleave_gate_value(b_conv, d_ff).reshape(1, -1), w_down16, S)
    return out.reshape(B, S, D)
```
